```python
import math
import jax, jax.numpy as jnp
from jax import lax
import numpy as np

D_MODEL = 1024
BATCH = 16
SEQ = 2048
DEPTH = 1

DN_HEAD_DIM = 128
DN_WIDTH = D_MODEL // 2
DN_HEADS = DN_WIDTH // DN_HEAD_DIM
SHORT_CONV = 4
GLA_VAL_DIM = 128
GLA_WIDTH = D_MODEL - DN_WIDTH
GLA_HEADS = GLA_WIDTH // GLA_VAL_DIM
GLA_KEY_DIM = GLA_VAL_DIM // 2
GLA_GATE_RANK = 16
GLA_TAU = 16.0
MIX_WIDTH = DN_WIDTH + GLA_WIDTH
IN_SPLITS = (DN_WIDTH, DN_WIDTH, DN_WIDTH, DN_WIDTH, DN_HEADS, DN_HEADS,
             GLA_HEADS * GLA_KEY_DIM, GLA_HEADS * GLA_KEY_DIM, GLA_WIDTH, GLA_WIDTH, GLA_GATE_RANK)
IN_WIDTH = sum(IN_SPLITS)
CHUNK = 64
D_FF = 2816
FFN_CONV = 3
ALPHA = (2.0 * DEPTH) ** 0.25
BETA_INIT = (8.0 * DEPTH) ** -0.25
EPS = 1e-6

kernel_name = "hybrid_deltanet_gla_convffn_deepnorm_adaln"


def layer_norm(x, g, b):
    xf = x.astype(jnp.float32)
    mu = xf.mean(-1, keepdims=True)
    var = jnp.square(xf - mu).mean(-1, keepdims=True)
    return ((xf - mu) * lax.rsqrt(var + EPS) * g.astype(jnp.float32) + b.astype(jnp.float32)).astype(x.dtype)


def rms_norm(x, g):
    xf = x.astype(jnp.float32)
    return (xf * lax.rsqrt(jnp.mean(xf * xf, -1, keepdims=True) + EPS) * g.astype(jnp.float32)).astype(x.dtype)


def l2_norm(x):
    return x * lax.rsqrt(jnp.sum(x * x, -1, keepdims=True) + EPS)


def causal_dwconv(x, w):
    k_w, ch = w.shape
    return lax.conv_general_dilated(x, w[:, None, :].astype(x.dtype), window_strides=(1,),
                                    padding=[(k_w - 1, 0)], dimension_numbers=('NWC', 'WIO', 'NWC'),
                                    feature_group_count=ch)


def to_chunks(x):
    b, t, h, d = x.shape
    return x.reshape(b, t // CHUNK, CHUNK, h, d).transpose(1, 0, 3, 2, 4)


def from_chunks(x):
    n, b, h, c, d = x.shape
    return x.transpose(1, 0, 3, 2, 4).reshape(b, n * c, h, d)


def gated_delta_rule(q, k, v, log_a, beta):
    dt = v.dtype
    q, k, v, log_a, beta = (a.astype(jnp.float32) for a in (q, k, v, log_a, beta))
    bsz, _, nh, dk = q.shape
    dv = v.shape[-1]
    qc, kc, vc = to_chunks(q * dk ** -0.5), to_chunks(k), to_chunks(v)
    G = jnp.cumsum(to_chunks(log_a[..., None])[..., 0], axis=-1)
    bc = to_chunks(beta[..., None])[..., 0]
    causal = jnp.tril(jnp.ones((CHUNK, CHUNK), dtype=bool))
    strict = jnp.tril(jnp.ones((CHUNK, CHUNK), dtype=bool), -1)
    decay = jnp.exp(jnp.where(causal, G[..., :, None] - G[..., None, :], -jnp.inf))
    kb = kc * bc[..., None]
    m_low = jnp.where(strict, jnp.einsum('nbhid,nbhjd->nbhij', kb, kc) * decay, 0.0)
    lhs = m_low + jnp.eye(CHUNK, dtype=jnp.float32)
    rhs = jnp.concatenate([vc * bc[..., None], kb * jnp.exp(G)[..., None]], axis=-1)
    sol = lax.linalg.triangular_solve(lhs, rhs, left_side=True, lower=True, unit_diagonal=True)
    u, w = sol[..., :dv], sol[..., dv:]
    attn = jnp.einsum('nbhid,nbhjd->nbhij', qc, kc) * decay
    q_dec = qc * jnp.exp(G)[..., None]
    k_dec = kc * jnp.exp(G[..., -1:] - G)[..., None]
    g_last = jnp.exp(G[..., -1])

    def step(S, inp):
        u_i, w_i, attn_i, qd_i, kd_i, gl_i = inp
        v_new = u_i - jnp.einsum('bhcd,bhde->bhce', w_i, S)
        o = jnp.einsum('bhcd,bhde->bhce', qd_i, S) + jnp.einsum('bhij,bhje->bhie', attn_i, v_new)
        S = S * gl_i[..., None, None] + jnp.einsum('bhcd,bhce->bhde', kd_i, v_new)
        return S, o

    S0 = jnp.zeros((bsz, nh, dk, dv), jnp.float32)
    _, o = lax.scan(step, S0, (u, w, attn, q_dec, k_dec, g_last))
    return from_chunks(o).astype(dt)


def gla_attention(q, k, v, log_alpha):
    dt = v.dtype
    q, k, v, log_alpha = (a.astype(jnp.float32) for a in (q, k, v, log_alpha))
    bsz, _, nh, dk = q.shape
    dv = v.shape[-1]
    qc, kc, vc = to_chunks(q * dk ** -0.5), to_chunks(k), to_chunks(v)
    b = jnp.cumsum(to_chunks(log_alpha), axis=-2)
    causal = jnp.tril(jnp.ones((CHUNK, CHUNK), dtype=bool))
    q_dec = qc * jnp.exp(b)
    attn = jnp.where(causal, jnp.einsum('nbhid,nbhjd->nbhij', q_dec, kc * jnp.exp(-b)), 0.0)
    o_intra = jnp.einsum('nbhij,nbhje->nbhie', attn, vc)
    k_dec = kc * jnp.exp(b[..., -1:, :] - b)
    g_last = jnp.exp(b[..., -1, :])

    def step(S, inp):
        qd_i, kd_i, v_i, oi_i, gl_i = inp
        o = jnp.einsum('bhcd,bhde->bhce', qd_i, S) + oi_i
        S = S * gl_i[..., :, None] + jnp.einsum('bhcd,bhce->bhde', kd_i, v_i)
        return S, o

    S0 = jnp.zeros((bsz, nh, dk, dv), jnp.float32)
    _, o = lax.scan(step, S0, (q_dec, k_dec, vc, o_intra, g_last))
    return from_chunks(o).astype(dt)


def hybrid_mixer(h, w_in, dn_conv, dn_a_log, dn_dt_bias, dn_norm_g, gla_w_gate2, gla_b_gate, gla_norm_g, w_o):
    bsz, t, _ = h.shape
    proj = h @ w_in
    offsets, acc = [], 0
    for s in IN_SPLITS[:-1]:
        acc += s
        offsets.append(acc)
    (dn_q, dn_k, dn_v, dn_z, dn_a, dn_b, gl_q, gl_k, gl_v, gl_g, gl_r) = jnp.split(proj, offsets, axis=-1)
    qkv = jax.nn.silu(causal_dwconv(jnp.concatenate([dn_q, dn_k, dn_v], -1), dn_conv))
    q, k, v = (a.reshape(bsz, t, DN_HEADS, DN_HEAD_DIM) for a in jnp.split(qkv, 3, axis=-1))
    log_a = -jnp.exp(dn_a_log) * jax.nn.softplus(dn_a + dn_dt_bias)
    beta = jax.nn.sigmoid(dn_b)
    o_dn = gated_delta_rule(l2_norm(q), l2_norm(k), v, log_a, beta)
    o_dn = rms_norm(o_dn, dn_norm_g) * jax.nn.silu(dn_z.reshape(bsz, t, DN_HEADS, DN_HEAD_DIM))
    log_alpha = jax.nn.log_sigmoid(gl_r @ gla_w_gate2 + gla_b_gate) / GLA_TAU
    o_gla = gla_attention(gl_q.reshape(bsz, t, GLA_HEADS, GLA_KEY_DIM),
                          gl_k.reshape(bsz, t, GLA_HEADS, GLA_KEY_DIM),
                          gl_v.reshape(bsz, t, GLA_HEADS, GLA_VAL_DIM),
                          log_alpha.reshape(bsz, t, GLA_HEADS, GLA_KEY_DIM))
    o_gla = rms_norm(o_gla, gla_norm_g) * jax.nn.silu(gl_g.reshape(bsz, t, GLA_HEADS, GLA_VAL_DIM))
    o = jnp.concatenate([o_dn.reshape(bsz, t, DN_WIDTH), o_gla.reshape(bsz, t, GLA_WIDTH)], axis=-1)
    return o @ w_o


def conv_ffn(h, w_up, conv_w, conv_b, w_down):
    u = causal_dwconv(h @ w_up, conv_w) + conv_b
    gate, val = jnp.split(u, 2, axis=-1)
    return (jax.nn.silu(gate) * val) @ w_down


def setup_inputs(seed: int = 0) -> dict:
    key = jax.random.key(seed)
    ks = jax.random.split(key, 24)
    nrm = lambda k, shape, s: jax.random.normal(k, shape, jnp.float32) * s
    L, D = DEPTH, D_MODEL
    dt = jnp.exp(jax.random.uniform(ks[9], (L, DN_HEADS), jnp.float32, math.log(1e-3), math.log(1e-1)))
    return {
        "x": nrm(ks[0], (BATCH, SEQ, D), 1.0),
        "c": nrm(ks[1], (BATCH, D), 1.0),
        "ln0_g": 1.0 + nrm(ks[2], (D,), 0.02),
        "ln0_b": nrm(ks[3], (D,), 0.02),
        "w_ada": nrm(ks[4], (L, D, 6 * D), 0.1 * D ** -0.5),
        "b_ada": nrm(ks[5], (L, 6 * D), 0.01),
        "w_in": nrm(ks[6], (L, D, IN_WIDTH), D ** -0.5),
        "dn_conv": nrm(ks[7], (L, SHORT_CONV, 3 * DN_WIDTH), SHORT_CONV ** -0.5),
        "dn_a_log": jnp.log(jax.random.uniform(ks[8], (L, DN_HEADS), jnp.float32, 1.0, 16.0)),
        "dn_dt_bias": dt + jnp.log(-jnp.expm1(-dt)),
        "dn_norm_g": 1.0 + nrm(ks[10], (L, DN_HEAD_DIM), 0.02),
        "gla_w_gate2": nrm(ks[11], (L, GLA_GATE_RANK, GLA_HEADS * GLA_KEY_DIM), GLA_GATE_RANK ** -0.5),
        "gla_b_gate": nrm(ks[12], (L, GLA_HEADS * GLA_KEY_DIM), 0.01),
        "gla_norm_g": 1.0 + nrm(ks[13], (L, GLA_VAL_DIM), 0.02),
        "w_o": nrm(ks[14], (L, MIX_WIDTH, D), MIX_WIDTH ** -0.5 * BETA_INIT),
        "ln1_g": 1.0 + nrm(ks[15], (L, D), 0.02),
        "ln1_b": nrm(ks[16], (L, D), 0.02),
        "ffn_w_up": nrm(ks[17], (L, D, 2 * D_FF), D ** -0.5),
        "ffn_conv": nrm(ks[18], (L, FFN_CONV, 2 * D_FF), FFN_CONV ** -0.5),
        "ffn_conv_b": nrm(ks[19], (L, 2 * D_FF), 0.02),
        "ffn_w_down": nrm(ks[20], (L, D_FF, D), D_FF ** -0.5 * BETA_INIT),
        "ln2_g": 1.0 + nrm(ks[21], (L, D), 0.02),
        "ln2_b": nrm(ks[22], (L, D), 0.02),
    }


def reference(x, c, ln0_g, ln0_b, w_ada, b_ada, w_in, dn_conv, dn_a_log, dn_dt_bias, dn_norm_g,
              gla_w_gate2, gla_b_gate, gla_norm_g, w_o, ln1_g, ln1_b, ffn_w_up, ffn_conv, ffn_conv_b,
              ffn_w_down, ln2_g, ln2_b):
    x = layer_norm(x, ln0_g, ln0_b)
    cond = jax.nn.silu(c)
    for l in range(DEPTH):
        mod = cond @ w_ada[l] + b_ada[l]
        sh_a, sc_a, gt_a, sh_f, sc_f, gt_f = jnp.split(mod[:, None, :], 6, axis=-1)
        h = x * (1.0 + sc_a) + sh_a
        y = hybrid_mixer(h, w_in[l], dn_conv[l], dn_a_log[l], dn_dt_bias[l], dn_norm_g[l],
                         gla_w_gate2[l], gla_b_gate[l], gla_norm_g[l], w_o[l])
        x = layer_norm(ALPHA * x + (1.0 + gt_a) * y, ln1_g[l], ln1_b[l])
        h = x * (1.0 + sc_f) + sh_f
        y = conv_ffn(h, ffn_w_up[l], ffn_conv[l], ffn_conv_b[l], ffn_w_down[l])
        x = layer_norm(ALPHA * x + (1.0 + gt_f) * y, ln2_g[l], ln2_b[l])
    return x
```

```python
import functools

import jax
import jax.numpy as jnp
from jax import lax
from jax.experimental import pallas as pl
from jax.experimental.pallas import tpu as pltpu

F32 = jnp.float32
BF16 = jnp.bfloat16

D_MODEL = 1024
DN_HEADS = 4
DN_DIM = 128
DN_WIDTH = DN_HEADS * DN_DIM
SHORT_CONV = 4
GLA_HEADS = 4
GLA_KEY = 64
GLA_VAL = 128
GLA_KW = GLA_HEADS * GLA_KEY
GLA_WIDTH = GLA_HEADS * GLA_VAL
GLA_RANK = 16
GLA_TAU = 16.0
CHUNK = 64
D_FF = 2816
FFN_CONV = 3
ALPHA = 2.0 ** 0.25
EPS = 1e-6

C_QKV = 0
C_Z = 3 * DN_WIDTH
C_GQ = C_Z + DN_WIDTH
C_GK = C_GQ + GLA_KW
C_GV = C_GK + GLA_KW
C_GG = C_GV + GLA_WIDTH
C_SM = C_GG + GLA_WIDTH
SMALL = 128
PROJ_W = C_SM + SMALL
SM_A, SM_B, SM_R = 0, DN_HEADS, 2 * DN_HEADS

CARRY = 8
VMEM_LIMIT = 56 * 1024 * 1024


def _bdot(a, b):
    return jnp.dot(a.astype(BF16), b.astype(BF16), preferred_element_type=F32)


def _bdot_nt(a, b):
    return lax.dot_general(a.astype(BF16), b.astype(BF16), (((1,), (1,)), ((), ())),
                           preferred_element_type=F32)


def _bdot_tn(a, b):
    return lax.dot_general(a.astype(BF16), b.astype(BF16), (((0,), (0,)), ((), ())),
                           preferred_element_type=F32)


def _split(a):
    hi = a.astype(BF16)
    lo = (a - hi.astype(F32)).astype(BF16)
    return hi, lo


def _dot3(a, b):
    ah, al = _split(a)
    bh, bl = _split(b)
    d = functools.partial(jnp.dot, preferred_element_type=F32)
    return d(ah, bh) + (d(ah, bl) + d(al, bh))


def _dot_exact_lhs(a_bf16, b):
    bh, bl = _split(b)
    d = functools.partial(jnp.dot, preferred_element_type=F32)
    return d(a_bf16, bh) + d(a_bf16, bl)


def _silu(x):
    return x * jax.nn.sigmoid(x)


def _layer_norm(x, g, b):
    mu = jnp.mean(x, axis=-1, keepdims=True)
    xc = x - mu
    var = jnp.mean(xc * xc, axis=-1, keepdims=True)
    return xc * lax.rsqrt(var + EPS) * g + b


def _ada_kernel(c_ref, w_ref, b_ref, o_ref):
    cond = _silu(c_ref[...])
    o_ref[...] = _bdot(cond, w_ref[...]) + b_ref[...]


def _ada(c, w_ada, b_ada):
    bsz = c.shape[0]
    n_out = w_ada.shape[1]
    blk = D_MODEL
    return pl.pallas_call(
        _ada_kernel,
        grid=(n_out // blk,),
        in_specs=[pl.BlockSpec((bsz, D_MODEL), lambda j: (0, 0)),
                  pl.BlockSpec((D_MODEL, blk), lambda j: (0, j)),
                  pl.BlockSpec((1, blk), lambda j: (0, j))],
        out_specs=pl.BlockSpec((bsz, blk), lambda j: (0, j)),
        out_shape=jax.ShapeDtypeStruct((bsz, n_out), F32),
        name="ada_mod",
    )(c, w_ada, b_ada.reshape(1, n_out))


def _inproj_kernel(x_ref, mod_ref, g_ref, b_ref, w_ref, o_ref):
    x0 = _layer_norm(x_ref[0], g_ref[...], b_ref[...])
    sh = mod_ref[0, :, 0:D_MODEL]
    sc = mod_ref[0, :, D_MODEL:2 * D_MODEL]
    h = x0 * (1.0 + sc) + sh
    o_ref[0] = jnp.dot(h.astype(BF16), w_ref[...], preferred_element_type=F32)


def _inproj(x, mod3, ln0_g, ln0_b, w_in_r, tm):
    bsz, t, _ = x.shape
    const = lambda b, i: (0, 0)
    return pl.pallas_call(
        _inproj_kernel,
        grid=(bsz, t // tm),
        in_specs=[pl.BlockSpec((1, tm, D_MODEL), lambda b, i: (b, i, 0)),
                  pl.BlockSpec((1, 1, 6 * D_MODEL), lambda b, i: (b, 0, 0)),
                  pl.BlockSpec((1, D_MODEL), const),
                  pl.BlockSpec((1, D_MODEL), const),
                  pl.BlockSpec((D_MODEL, PROJ_W), const)],
        out_specs=pl.BlockSpec((1, tm, PROJ_W), lambda b, i: (b, i, 0)),
        out_shape=jax.ShapeDtypeStruct((bsz, t, PROJ_W), F32),
        compiler_params=pltpu.CompilerParams(
            dimension_semantics=("arbitrary", "arbitrary"), vmem_limit_bytes=VMEM_LIMIT),
        name="inproj",
    )(x, mod3, ln0_g.reshape(1, -1), ln0_b.reshape(1, -1), w_in_r)


def _neumann_inverse(m_low):
    n = m_low.shape[0]
    eye = (lax.broadcasted_iota(jnp.int32, (n, n), 0)
           == lax.broadcasted_iota(jnp.int32, (n, n), 1)).astype(F32)
    x = -m_low
    t = eye + x
    p = x
    steps = 1
    while steps * 2 < n:
        steps *= 2
        p = _dot3(p, p)
        t = t + _dot3(t, p)
    return t


def _mixer_kernel(proj_ref, tri_ref, cw_ref, alog_ref, dtb_ref, dng_ref, wg2_ref, bg_ref, glg_ref,
                  o_ref,
                  xe_scr, qn_scr, kn_scr, v_scr, sc_scr, gb_scr, sdn_scr, sgl_scr, *, tt):
    t_idx = pl.program_id(1)

    @pl.when(t_idx == 0)
    def _():
        xe_scr[0:CARRY, :] = jnp.zeros((CARRY, 3 * DN_WIDTH), F32)
        sdn_scr[...] = jnp.zeros_like(sdn_scr)
        sgl_scr[...] = jnp.zeros_like(sgl_scr)

    xe_scr[CARRY:CARRY + tt, :] = proj_ref[0, :, C_QKV:C_QKV + 3 * DN_WIDTH]
    for j in range(3 * DN_HEADS):
        cs = slice(j * DN_DIM, (j + 1) * DN_DIM)
        conv = None
        for k in range(SHORT_CONV):
            off = CARRY - (SHORT_CONV - 1) + k
            term = xe_scr[off:off + tt, cs] * cw_ref[k:k + 1, cs]
            conv = term if conv is None else conv + term
        act = _silu(conv)
        hs = slice((j % DN_HEADS) * DN_DIM, (j % DN_HEADS + 1) * DN_DIM)
        if j < DN_HEADS:
            qn_scr[:, hs] = act * (lax.rsqrt(jnp.sum(act * act, -1, keepdims=True) + EPS)
                                   * (DN_DIM ** -0.5))
        elif j < 2 * DN_HEADS:
            kn_scr[:, hs] = act * lax.rsqrt(jnp.sum(act * act, -1, keepdims=True) + EPS)
        else:
            v_scr[:, hs] = act
    xe_scr[0:CARRY, :] = xe_scr[tt:tt + CARRY, :]

    small = proj_ref[0, :, C_SM:C_SM + SMALL]
    lane = lax.broadcasted_iota(jnp.int32, (tt, SMALL), 1)
    log_a = -jnp.exp(alog_ref[...]) * jax.nn.softplus(small + dtb_ref[...])
    log_a = jnp.where(lane < DN_HEADS, log_a, 0.0)
    g_cum = _dot_exact_lhs(tri_ref[...], log_a)
    beta = jax.nn.sigmoid(small)
    sc_scr[...] = jnp.where(lane < DN_HEADS, g_cum, beta)

    gate_pre = _bdot(small, wg2_ref[...]) + bg_ref[...]
    log_alpha = jax.nn.log_sigmoid(gate_pre) * (1.0 / GLA_TAU)
    gb_scr[...] = _dot_exact_lhs(tri_ref[...], log_alpha)

    row = lax.broadcasted_iota(jnp.int32, (CHUNK, CHUNK), 0)
    col = lax.broadcasted_iota(jnp.int32, (CHUNK, CHUNK), 1)
    causal = row >= col
    strict = row > col

    def chunk_body(c, carry):
        r0 = pl.multiple_of(c * CHUNK, CHUNK)
        rows = pl.ds(r0, CHUNK)
        sc = sc_scr[rows, :]
        sc_t = sc.T
        gb_t = gb_scr[rows, :].T
        for h in range(DN_HEADS):
            hs = slice(h * DN_DIM, (h + 1) * DN_DIM)
            g = sc[:, h:h + 1]
            g_row = sc_t[h:h + 1, :]
            bet = sc[:, DN_HEADS + h:DN_HEADS + h + 1]
            g_last = g[CHUNK - 1:CHUNK, :]
            qn = qn_scr[rows, hs]
            kn = kn_scr[rows, hs]
            vv = v_scr[rows, hs]
            decay = jnp.exp(jnp.where(causal, g - g_row, -jnp.inf))
            kb = kn * bet
            m_low = jnp.where(strict, _bdot_nt(kb, kn) * decay, 0.0)
            t_inv = _neumann_inverse(m_low)
            attn = _bdot_nt(qn, kn) * decay
            eg = jnp.exp(g)
            rhs = jnp.concatenate([vv * bet, kb * eg], axis=-1)
            sol = _dot3(t_inv, rhs)
            u = sol[:, :DN_DIM]
            w = sol[:, DN_DIM:]
            s = sdn_scr[h]
            v_new = u - _bdot(w, s)
            o = _bdot(qn * eg, s) + _bdot(attn, v_new)
            k_dec = kn * jnp.exp(g_last - g)
            sdn_scr[h] = s * jnp.exp(g_last) + _bdot_tn(k_dec, v_new)
            z = proj_ref[0, rows, C_Z + h * DN_DIM:C_Z + (h + 1) * DN_DIM]
            on = o * lax.rsqrt(jnp.mean(o * o, -1, keepdims=True) + EPS) * dng_ref[...]
            o_ref[0, rows, hs] = (on * _silu(z)).astype(o_ref.dtype)
        for h in range(GLA_HEADS):
            ks = slice(h * GLA_KEY, (h + 1) * GLA_KEY)
            vs = slice(h * GLA_VAL, (h + 1) * GLA_VAL)
            b = gb_scr[rows, ks]
            q = proj_ref[0, rows, C_GQ + h * GLA_KEY:C_GQ + (h + 1) * GLA_KEY] * (GLA_KEY ** -0.5)
            k = proj_ref[0, rows, C_GK + h * GLA_KEY:C_GK + (h + 1) * GLA_KEY]
            vv = proj_ref[0, rows, C_GV + h * GLA_VAL:C_GV + (h + 1) * GLA_VAL]
            b_last = b[CHUNK - 1:CHUNK, :]
            q_dec = q * jnp.exp(b)
            attn = jnp.where(causal, _bdot_nt(q_dec, k * jnp.exp(-b)), 0.0)
            s = sgl_scr[h]
            o = _bdot(q_dec, s) + _bdot(attn, vv)
            k_dec = k * jnp.exp(b_last - b)
            sgl_scr[h] = s * jnp.exp(gb_t[ks, CHUNK - 1:CHUNK]) + _bdot_tn(k_dec, vv)
            gg = proj_ref[0, rows, C_GG + h * GLA_VAL:C_GG + (h + 1) * GLA_VAL]
            on = o * lax.rsqrt(jnp.mean(o * o, -1, keepdims=True) + EPS) * glg_ref[...]
            o_ref[0, rows, DN_WIDTH + h * GLA_VAL:DN_WIDTH + (h + 1) * GLA_VAL] = (
                on * _silu(gg)).astype(o_ref.dtype)
        return carry

    lax.fori_loop(0, tt // CHUNK, chunk_body, 0)


def _mixer(proj, tri, dn_conv, alog_p, dtb_p, dn_norm_g, wg2_p, b_gate, gla_norm_g, tt):
    bsz, t, _ = proj.shape
    const = lambda b, i: (0, 0)
    return pl.pallas_call(
        functools.partial(_mixer_kernel, tt=tt),
        grid=(bsz, t // tt),
        in_specs=[pl.BlockSpec((1, tt, PROJ_W), lambda b, i: (b, i, 0)),
                  pl.BlockSpec((tt, tt), const),
                  pl.BlockSpec((SHORT_CONV, 3 * DN_WIDTH), const),
                  pl.BlockSpec((1, SMALL), const),
                  pl.BlockSpec((1, SMALL), const),
                  pl.BlockSpec((1, DN_DIM), const),
                  pl.BlockSpec((SMALL, GLA_KW), const),
                  pl.BlockSpec((1, GLA_KW), const),
                  pl.BlockSpec((1, GLA_VAL), const)],
        out_specs=pl.BlockSpec((1, tt, D_MODEL), lambda b, i: (b, i, 0)),
        out_shape=jax.ShapeDtypeStruct((bsz, t, D_MODEL), BF16),
        scratch_shapes=[pltpu.VMEM((tt + CARRY, 3 * DN_WIDTH), F32),
                        pltpu.VMEM((tt, DN_WIDTH), F32),
                        pltpu.VMEM((tt, DN_WIDTH), F32),
                        pltpu.VMEM((tt, DN_WIDTH), F32),
                        pltpu.VMEM((tt, SMALL), F32),
                        pltpu.VMEM((tt, GLA_KW), F32),
                        pltpu.VMEM((DN_HEADS, DN_DIM, DN_DIM), F32),
                        pltpu.VMEM((GLA_HEADS, GLA_KEY, GLA_VAL), F32)],
        compiler_params=pltpu.CompilerParams(
            dimension_semantics=("arbitrary", "arbitrary"), vmem_limit_bytes=VMEM_LIMIT),
        name="mixer",
    )(proj, tri, dn_conv, alog_p, dtb_p, dn_norm_g.reshape(1, -1), wg2_p,
      b_gate.reshape(1, -1), gla_norm_g.reshape(1, -1))


def _ffn_kernel(x_ref, o_ref, mod_ref, g0_ref, b0_ref, wo_ref, g1_ref, b1_ref,
                wup_ref, cw_ref, cb_ref, wdn_ref, g2_ref, b2_ref,
                out_ref, ue_scr, *, tf, fc):
    t_idx = pl.program_id(1)

    @pl.when(t_idx == 0)
    def _():
        ue_scr[:, 0:CARRY, :] = jnp.zeros((ue_scr.shape[0], CARRY, ue_scr.shape[2]), F32)

    gt_a = mod_ref[0, :, 2 * D_MODEL:3 * D_MODEL]
    sh_f = mod_ref[0, :, 3 * D_MODEL:4 * D_MODEL]
    sc_f = mod_ref[0, :, 4 * D_MODEL:5 * D_MODEL]
    gt_f = mod_ref[0, :, 5 * D_MODEL:6 * D_MODEL]

    x0 = _layer_norm(x_ref[0], g0_ref[...], b0_ref[...])
    y = jnp.dot(o_ref[0], wo_ref[...], preferred_element_type=F32)
    x1 = _layer_norm(ALPHA * x0 + (1.0 + gt_a) * y, g1_ref[...], b1_ref[...])
    h2 = (x1 * (1.0 + sc_f) + sh_f).astype(BF16)

    n_chunks = D_FF // fc
    y2 = None
    for j in range(n_chunks):
        halves = []
        for part in range(2):
            c0 = part * D_FF + j * fc
            slot = 2 * j + part
            ue_scr[slot, CARRY:CARRY + tf, :] = jnp.dot(
                h2, wup_ref[:, c0:c0 + fc], preferred_element_type=F32)
            acc = cb_ref[:, c0:c0 + fc]
            for k in range(FFN_CONV):
                off = CARRY - (FFN_CONV - 1) + k
                acc = acc + ue_scr[slot, off:off + tf, :] * cw_ref[k:k + 1, c0:c0 + fc]
            ue_scr[slot, 0:CARRY, :] = ue_scr[slot, tf:tf + CARRY, :]
            halves.append(acc)
        act = (_silu(halves[0]) * halves[1]).astype(BF16)
        part_y = jnp.dot(act, wdn_ref[j * fc:(j + 1) * fc, :], preferred_element_type=F32)
        y2 = part_y if y2 is None else y2 + part_y
    out_ref[0] = _layer_norm(ALPHA * x1 + (1.0 + gt_f) * y2, g2_ref[...], b2_ref[...])


def _ffn(x, o, mod3, ln0_g, ln0_b, w_o, ln1_g, ln1_b, w_up, conv_w, conv_b, w_down,
         ln2_g, ln2_b, tf, fc):
    bsz, t, _ = x.shape
    const = lambda b, i: (0, 0)
    once = pl.Buffered(1)
    vec = lambda a: a.reshape(1, -1)
    return pl.pallas_call(
        functools.partial(_ffn_kernel, tf=tf, fc=fc),
        grid=(bsz, t // tf),
        in_specs=[pl.BlockSpec((1, tf, D_MODEL), lambda b, i: (b, i, 0)),
                  pl.BlockSpec((1, tf, D_MODEL), lambda b, i: (b, i, 0)),
                  pl.BlockSpec((1, 1, 6 * D_MODEL), lambda b, i: (b, 0, 0)),
                  pl.BlockSpec((1, D_MODEL), const),
                  pl.BlockSpec((1, D_MODEL), const),
                  pl.BlockSpec((D_MODEL, D_MODEL), const, pipeline_mode=once),
                  pl.BlockSpec((1, D_MODEL), const),
                  pl.BlockSpec((1, D_MODEL), const),
                  pl.BlockSpec((D_MODEL, 2 * D_FF), const, pipeline_mode=once),
                  pl.BlockSpec((FFN_CONV, 2 * D_FF), const),
                  pl.BlockSpec((1, 2 * D_FF), const),
                  pl.BlockSpec((D_FF, D_MODEL), const, pipeline_mode=once),
                  pl.BlockSpec((1, D_MODEL), const),
                  pl.BlockSpec((1, D_MODEL), const)],
        out_specs=pl.BlockSpec((1, tf, D_MODEL), lambda b, i: (b, i, 0)),
        out_shape=jax.ShapeDtypeStruct((bsz, t, D_MODEL), F32),
        scratch_shapes=[pltpu.VMEM((2 * (D_FF // fc), tf + CARRY, fc), F32)],
        compiler_params=pltpu.CompilerParams(
            dimension_semantics=("arbitrary", "arbitrary"), vmem_limit_bytes=VMEM_LIMIT),
        name="ffn",
    )(x, o, mod3, vec(ln0_g), vec(ln0_b), w_o, vec(ln1_g), vec(ln1_b), w_up, conv_w,
      vec(conv_b), w_down, vec(ln2_g), vec(ln2_b))


def _pad_lanes(a, offset, width):
    a2 = a.reshape(1, -1) if a.ndim == 1 else a
    return jnp.pad(a2, ((0, 0), (offset, width - offset - a2.shape[1])))


def kernel(x, c, ln0_g, ln0_b, w_ada, b_ada, w_in, dn_conv, dn_a_log, dn_dt_bias, dn_norm_g,
           gla_w_gate2, gla_b_gate, gla_norm_g, w_o, ln1_g, ln1_b, ffn_w_up, ffn_conv, ffn_conv_b,
           ffn_w_down, ln2_g, ln2_b):
    bsz, t, _ = x.shape
    tm, tt, tf, fc = 512, 512, 512, 1408

    wi = w_in[0]
    o_q, o_k, o_v, o_z = 0, DN_WIDTH, 2 * DN_WIDTH, 3 * DN_WIDTH
    o_a = 4 * DN_WIDTH
    o_b = o_a + DN_HEADS
    o_gq = o_b + DN_HEADS
    o_gk = o_gq + GLA_KW
    o_gv = o_gk + GLA_KW
    o_gg = o_gv + GLA_WIDTH
    o_gr = o_gg + GLA_WIDTH
    small_cols = jnp.concatenate(
        [wi[:, o_a:o_a + DN_HEADS], wi[:, o_b:o_b + DN_HEADS], wi[:, o_gr:o_gr + GLA_RANK],
         jnp.zeros((D_MODEL, SMALL - 2 * DN_HEADS - GLA_RANK), wi.dtype)], axis=1)
    w_in_r = jnp.concatenate([wi[:, o_q:o_a], wi[:, o_gq:o_gr], small_cols], axis=1).astype(BF16)
    alog_p = _pad_lanes(dn_a_log[0], SM_A, SMALL)
    dtb_p = _pad_lanes(dn_dt_bias[0], SM_A, SMALL)
    wg2_p = jnp.pad(gla_w_gate2[0], ((SM_R, SMALL - SM_R - GLA_RANK), (0, 0))).astype(BF16)
    idx = jnp.arange(tt)
    tri = ((idx[:, None] >= idx[None, :])
           & (idx[:, None] // CHUNK == idx[None, :] // CHUNK)).astype(BF16)

    mod = _ada(c, w_ada[0], b_ada[0])
    mod3 = mod.reshape(bsz, 1, 6 * D_MODEL)
    proj = _inproj(x, mod3, ln0_g, ln0_b, w_in_r, tm)
    o = _mixer(proj, tri, dn_conv[0], alog_p, dtb_p, dn_norm_g[0], wg2_p, gla_b_gate[0],
               gla_norm_g[0], tt)
    return _ffn(x, o, mod3, ln0_g, ln0_b, w_o[0].astype(BF16), ln1_g[0], ln1_b[0],
                ffn_w_up[0].astype(BF16), ffn_conv[0], ffn_conv_b[0],
                ffn_w_down[0].astype(BF16), ln2_g[0], ln2_b[0], tf, fc)
```

```python
import functools

import jax
import jax.numpy as jnp
import numpy as np
from jax import lax
from jax.experimental import pallas as pl
from jax.experimental.pallas import tpu as pltpu

F32 = jnp.float32
BF16 = jnp.bfloat16

D_MODEL = 1024
DN_HEADS = 4
DN_DIM = 128
DN_WIDTH = DN_HEADS * DN_DIM
SHORT_CONV = 4
GLA_HEADS = 4
GLA_KEY = 64
GLA_VAL = 128
GLA_KW = GLA_HEADS * GLA_KEY
GLA_WIDTH = GLA_HEADS * GLA_VAL
GLA_RANK = 16
GLA_TAU = 16.0
CHUNK = 64
D_FF = 2816
FFN_CONV = 3
ALPHA = 2.0 ** 0.25
EPS = 1e-6
NH = 4
QUAD = NH * CHUNK

C_QKV = 0
C_Z = 3 * DN_WIDTH
C_GQ = C_Z + DN_WIDTH
C_GK = C_GQ + GLA_KW
C_GV = C_GK + GLA_KW
C_GG = C_GV + GLA_WIDTH
C_SM = C_GG + GLA_WIDTH
SMALL = 128
PROJ_W = C_SM + SMALL
SM_A, SM_B, SM_R = 0, DN_HEADS, 2 * DN_HEADS

BC_GQ = 0
BC_G = QUAD
BC_B = BC_G + DN_WIDTH
BC_W = BC_B + DN_WIDTH

CARRY = 8
VMEM_LIMIT = 56 * 1024 * 1024


def _dot(a, b):
    return jnp.dot(a, b, preferred_element_type=F32)


def _dot_nt(a, b):
    return lax.dot_general(a, b, (((1,), (1,)), ((), ())), preferred_element_type=F32)


def _dot_tn(a, b):
    return lax.dot_general(a, b, (((0,), (0,)), ((), ())), preferred_element_type=F32)


def _bdot(a, b):
    return _dot(a.astype(BF16), b.astype(BF16))


def _split(a):
    hi = a.astype(BF16)
    lo = (a - hi.astype(F32)).astype(BF16)
    return hi, lo


def _dot_exact_lhs(a_bf16, b):
    bh, bl = _split(b)
    return _dot(a_bf16, bh) + _dot(a_bf16, bl)


def _silu(x):
    return x * jax.nn.sigmoid(x)


def _layer_norm(x, g, b):
    mu = jnp.mean(x, axis=-1, keepdims=True)
    xc = x - mu
    var = jnp.mean(xc * xc, axis=-1, keepdims=True)
    return xc * lax.rsqrt(var + EPS) * g + b


def _rows_by_head(a, width):
    return jnp.concatenate([a[:, h * width:(h + 1) * width] for h in range(NH)], axis=0)


def _stack4(a):
    return jnp.concatenate([a, a, a, a], axis=0)


def _ada_kernel(c_ref, w_ref, b_ref, o_ref):
    cond = _silu(c_ref[...])
    o_ref[...] = _bdot(cond, w_ref[...]) + b_ref[...]


def _ada(c, w_ada, b_ada):
    bsz = c.shape[0]
    n_out = w_ada.shape[1]
    blk = D_MODEL
    return pl.pallas_call(
        _ada_kernel,
        grid=(n_out // blk,),
        in_specs=[pl.BlockSpec((bsz, D_MODEL), lambda j: (0, 0)),
                  pl.BlockSpec((D_MODEL, blk), lambda j: (0, j)),
                  pl.BlockSpec((1, blk), lambda j: (0, j))],
        out_specs=pl.BlockSpec((bsz, blk), lambda j: (0, j)),
        out_shape=jax.ShapeDtypeStruct((bsz, n_out), F32),
        name="ada_mod",
    )(c, w_ada, b_ada.reshape(1, n_out))


def _inproj_kernel(x_ref, mod_ref, g_ref, b_ref, w_ref, o_ref):
    x0 = _layer_norm(x_ref[0], g_ref[...], b_ref[...])
    sh = mod_ref[0, :, 0:D_MODEL]
    sc = mod_ref[0, :, D_MODEL:2 * D_MODEL]
    h = x0 * (1.0 + sc) + sh
    o_ref[0] = _dot(h.astype(BF16), w_ref[...])


def _inproj(x, mod3, ln0_g, ln0_b, w_in_r, tm):
    bsz, t, _ = x.shape
    const = lambda b, i: (0, 0)
    return pl.pallas_call(
        _inproj_kernel,
        grid=(bsz, t // tm),
        in_specs=[pl.BlockSpec((1, tm, D_MODEL), lambda b, i: (b, i, 0)),
                  pl.BlockSpec((1, 1, 6 * D_MODEL), lambda b, i: (b, 0, 0)),
                  pl.BlockSpec((1, D_MODEL), const),
                  pl.BlockSpec((1, D_MODEL), const),
                  pl.BlockSpec((D_MODEL, PROJ_W), const)],
        out_specs=pl.BlockSpec((1, tm, PROJ_W), lambda b, i: (b, i, 0)),
        out_shape=jax.ShapeDtypeStruct((bsz, t, PROJ_W), F32),
        compiler_params=pltpu.CompilerParams(
            dimension_semantics=("arbitrary", "arbitrary"), vmem_limit_bytes=VMEM_LIMIT),
        name="inproj",
    )(x, mod3, ln0_g.reshape(1, -1), ln0_b.reshape(1, -1), w_in_r)


def _mixer_kernel(proj_ref, tri_ref, sel_ref, mq_ref, mw_ref, mr_ref, cw_ref, alog_ref, dtb_ref,
                  dng_ref, wg2_ref, bg_ref, glg_ref,
                  o_ref,
                  xe_scr, qn_scr, kn_scr, v_scr, sl_scr, bc_scr, gb_scr,
                  w_scr, u_scr, at_scr, qd_scr, kd_scr, sdn_scr, sgl_scr, *, tt):
    t_idx = pl.program_id(1)
    n_chunks = tt // CHUNK

    @pl.when(t_idx == 0)
    def _():
        xe_scr[0:CARRY, :] = jnp.zeros((CARRY, 3 * DN_WIDTH), F32)
        sdn_scr[...] = jnp.zeros_like(sdn_scr)
        sgl_scr[...] = jnp.zeros_like(sgl_scr)

    xe_scr[CARRY:CARRY + tt, :] = proj_ref[0, :, C_QKV:C_QKV + 3 * DN_WIDTH]
    for j in range(3 * DN_HEADS):
        cs = slice(j * DN_DIM, (j + 1) * DN_DIM)
        conv = None
        for k in range(SHORT_CONV):
            off = CARRY - (SHORT_CONV - 1) + k
            term = xe_scr[off:off + tt, cs] * cw_ref[k:k + 1, cs]
            conv = term if conv is None else conv + term
        act = _silu(conv)
        hs = slice((j % DN_HEADS) * DN_DIM, (j % DN_HEADS + 1) * DN_DIM)
        if j < DN_HEADS:
            qn_scr[:, hs] = act * (lax.rsqrt(jnp.sum(act * act, -1, keepdims=True) + EPS)
                                   * (DN_DIM ** -0.5))
        elif j < 2 * DN_HEADS:
            kn_scr[:, hs] = act * lax.rsqrt(jnp.sum(act * act, -1, keepdims=True) + EPS)
        else:
            v_scr[:, hs] = act
    xe_scr[0:CARRY, :] = xe_scr[tt:tt + CARRY, :]

    small = proj_ref[0, :, C_SM:C_SM + SMALL]
    lane = lax.broadcasted_iota(jnp.int32, (tt, SMALL), 1)
    log_a = -jnp.exp(alog_ref[...]) * jax.nn.softplus(small + dtb_ref[...])
    log_a = jnp.where(lane < DN_HEADS, log_a, 0.0)
    g_cum = _dot_exact_lhs(tri_ref[...], log_a)
    sc = jnp.where(lane < DN_HEADS, g_cum, jax.nn.sigmoid(small))
    sc_hi, sc_lo = _split(sc)
    sl = jnp.concatenate([sc_hi, sc_lo], axis=1)
    sl_scr[...] = sl
    bc_scr[...] = _dot(sl, sel_ref[...])

    gate_pre = _bdot(small, wg2_ref[...]) + bg_ref[...]
    log_alpha = jax.nn.log_sigmoid(gate_pre) * (1.0 / GLA_TAU)
    gb_scr[...] = _dot_exact_lhs(tri_ref[...], log_alpha)

    def quad_masks():
        row = lax.broadcasted_iota(jnp.int32, (CHUNK, QUAD), 0)
        col = lax.broadcasted_iota(jnp.int32, (CHUNK, QUAD), 1) & (CHUNK - 1)
        return row, col

    def rows_of(c):
        return pl.ds(pl.multiple_of(c * CHUNK, CHUNK), CHUNK)

    def block_diag(a_bf16):
        return _stack4(a_bf16) * mq_ref[...]

    def dn_prepare(c):
        rows = rows_of(c)
        row, col = quad_masks()
        kn = kn_scr[rows, :]
        qn = qn_scr[rows, :]
        vv = v_scr[rows, :]
        g_q = bc_scr[rows, BC_GQ:BC_GQ + QUAD]
        g_w = bc_scr[rows, BC_G:BC_G + DN_WIDTH]
        b_w = bc_scr[rows, BC_B:BC_B + DN_WIDTH]
        r_mat = _stack4(sl_scr[rows, :]) * mr_ref[...]
        g_row = _dot_nt(jnp.ones((CHUNK, 2 * SMALL), BF16), r_mat)
        decay = jnp.exp(jnp.where(row >= col, g_q - g_row, -jnp.inf))
        kb = kn * b_w
        k_bd = _stack4(kn.astype(BF16)) * mw_ref[...]
        aqk = _dot_nt(jnp.concatenate([kb, qn], axis=0).astype(BF16), k_bd)
        m_low = jnp.where(row > col, aqk[0:CHUNK] * decay, 0.0)
        attn = aqk[CHUNK:2 * CHUNK] * decay
        p = -m_low
        t = jnp.where(row == col, 1.0, 0.0) + p
        n_levels = CHUNK.bit_length() - 1
        for lvl in range(n_levels):
            bd = block_diag(p.astype(BF16))
            if lvl == 0:
                p = _dot(p.astype(BF16), bd)
            elif lvl < n_levels - 1:
                both = _dot(jnp.concatenate([p, t], axis=0).astype(BF16), bd)
                p = both[0:CHUNK]
                t = t + both[CHUNK:2 * CHUNK]
            else:
                t = t + _dot(t.astype(BF16), bd)
        eg = jnp.exp(g_w)
        kg = (kb * eg).astype(BF16)
        vb = (vv * b_w).astype(BF16)
        rhs = jnp.concatenate(
            [jnp.concatenate([kg[:, h * DN_DIM:(h + 1) * DN_DIM],
                              vb[:, h * DN_DIM:(h + 1) * DN_DIM]], axis=1) for h in range(NH)],
            axis=0)
        wu = _dot(block_diag(t.astype(BF16)), rhs)
        w_scr[c] = wu[:, 0:DN_DIM].astype(BF16)
        u_scr[c] = wu[:, DN_DIM:2 * DN_DIM]
        at_scr[c] = block_diag(attn.astype(BF16))
        qd_scr[c] = (qn * eg).astype(BF16)
        g_last = g_w[CHUNK - 1:CHUNK, :]
        kd_scr[c] = (kn * jnp.exp(g_last - g_w)).astype(BF16)

    def dn_apply(c):
        rows = rows_of(c)
        w_all = w_scr[c]
        qd = qd_scr[c]
        ws, qs, states = [], [], []
        for h in range(NH):
            s = sdn_scr[h]
            lhs = jnp.concatenate([w_all[h * CHUNK:(h + 1) * CHUNK, :],
                                   qd[:, h * DN_DIM:(h + 1) * DN_DIM]], axis=0)
            r = _dot(lhs, s.astype(BF16))
            ws.append(r[0:CHUNK])
            qs.append(r[CHUNK:2 * CHUNK])
            states.append(s)
        v_new = u_scr[c] - jnp.concatenate(ws, axis=0)
        v_new_b = v_new.astype(BF16)
        o = jnp.concatenate(qs, axis=0) + _dot(at_scr[c], v_new_b)
        last8 = pl.ds(pl.multiple_of(c * CHUNK + CHUNK - CARRY, CARRY), CARRY)
        g_last = bc_scr[last8, BC_G:BC_G + DN_WIDTH][CARRY - 1:CARRY, :]
        eg_last = jnp.exp(g_last)
        kd = kd_scr[c]
        for h in range(NH):
            hs = slice(h * DN_DIM, (h + 1) * DN_DIM)
            sdn_scr[h] = states[h] * eg_last[:, hs] + _dot_tn(
                kd[:, hs], v_new_b[h * CHUNK:(h + 1) * CHUNK, :])
        z = _rows_by_head(proj_ref[0, rows, C_Z:C_Z + DN_WIDTH], DN_DIM)
        on = o * lax.rsqrt(jnp.mean(o * o, -1, keepdims=True) + EPS) * dng_ref[...]
        res = (on * _silu(z)).astype(o_ref.dtype)
        for h in range(NH):
            o_ref[0, rows, h * DN_DIM:(h + 1) * DN_DIM] = res[h * CHUNK:(h + 1) * CHUNK, :]

    def gla_chunk(c):
        rows = rows_of(c)
        row, col = quad_masks()
        b = gb_scr[rows, :]
        q = proj_ref[0, rows, C_GQ:C_GQ + GLA_KW] * (GLA_KEY ** -0.5)
        k = proj_ref[0, rows, C_GK:C_GK + GLA_KW]
        v = proj_ref[0, rows, C_GV:C_GV + GLA_WIDTH].astype(BF16)
        b_last = b[CHUNK - 1:CHUNK, :]
        q_dec = (q * jnp.exp(b)).astype(BF16)
        k_inv = (k * jnp.exp(-b)).astype(BF16)
        k_dec = (k * jnp.exp(b_last - b)).astype(BF16)
        attn = jnp.where(row >= col, _dot_nt(q_dec, block_diag(k_inv)), 0.0)
        st = sgl_scr[...]
        o = (_dot_nt(block_diag(q_dec), st.astype(BF16))
             + _dot(block_diag(attn.astype(BF16)), _rows_by_head(v, GLA_VAL)))
        kt = _dot_tn(v, k_dec)
        lane_head = lax.broadcasted_iota(jnp.int32, (GLA_VAL, GLA_KW), 1) // GLA_KEY
        upd = kt[0:GLA_VAL]
        for h in range(1, NH):
            upd = jnp.where(lane_head == h, kt[h * GLA_VAL:(h + 1) * GLA_VAL], upd)
        sgl_scr[...] = st * jnp.exp(b_last) + upd
        gg = _rows_by_head(proj_ref[0, rows, C_GG:C_GG + GLA_WIDTH], GLA_VAL)
        on = o * lax.rsqrt(jnp.mean(o * o, -1, keepdims=True) + EPS) * glg_ref[...]
        res = (on * _silu(gg)).astype(o_ref.dtype)
        for h in range(NH):
            o_ref[0, rows, DN_WIDTH + h * GLA_VAL:DN_WIDTH + (h + 1) * GLA_VAL] = (
                res[h * CHUNK:(h + 1) * CHUNK, :])

    dn_prepare(0)

    def body(c, carry):
        dn_apply(c)
        dn_prepare(c + 1)
        gla_chunk(c)
        return carry

    lax.fori_loop(0, n_chunks - 1, body, 0)
    dn_apply(n_chunks - 1)
    gla_chunk(n_chunks - 1)


def _mixer_constants(tt):
    idx = np.arange(tt)
    tri = (idx[:, None] >= idx[None, :]) & (idx[:, None] // CHUNK == idx[None, :] // CHUNK)
    sel = np.zeros((2 * SMALL, BC_W), np.float32)
    for part in range(2):
        for h in range(NH):
            sel[part * SMALL + SM_A + h, BC_GQ + h * CHUNK:BC_GQ + (h + 1) * CHUNK] = 1.0
            sel[part * SMALL + SM_A + h, BC_G + h * DN_DIM:BC_G + (h + 1) * DN_DIM] = 1.0
            sel[part * SMALL + SM_B + h, BC_B + h * DN_DIM:BC_B + (h + 1) * DN_DIM] = 1.0
    r = np.arange(QUAD)
    mask_q = (r[:, None] // CHUNK == r[None, :] // CHUNK)
    mask_w = (r[:, None] // CHUNK == np.arange(DN_WIDTH)[None, :] // DN_DIM)
    l2 = np.arange(2 * SMALL)
    mask_r = (l2[None, :] % SMALL == SM_A + r[:, None] // CHUNK)
    as_bf16 = lambda a: jnp.asarray(a.astype(np.float32), dtype=BF16)
    return as_bf16(tri), as_bf16(sel), as_bf16(mask_q), as_bf16(mask_w), as_bf16(mask_r)


def _mixer(proj, dn_conv, alog_p, dtb_p, dn_norm_g, wg2_p, b_gate, gla_norm_g, tt):
    bsz, t, _ = proj.shape
    n_chunks = tt // CHUNK
    const = lambda b, i: (0, 0)
    tri, sel, mask_q, mask_w, mask_r = _mixer_constants(tt)
    return pl.pallas_call(
        functools.partial(_mixer_kernel, tt=tt),
        grid=(bsz, t // tt),
        in_specs=[pl.BlockSpec((1, tt, PROJ_W), lambda b, i: (b, i, 0)),
                  pl.BlockSpec((tt, tt), const),
                  pl.BlockSpec((2 * SMALL, BC_W), const),
                  pl.BlockSpec((QUAD, QUAD), const),
                  pl.BlockSpec((QUAD, DN_WIDTH), const),
                  pl.BlockSpec((QUAD, 2 * SMALL), const),
                  pl.BlockSpec((SHORT_CONV, 3 * DN_WIDTH), const),
                  pl.BlockSpec((1, SMALL), const),
                  pl.BlockSpec((1, SMALL), const),
                  pl.BlockSpec((1, DN_DIM), const),
                  pl.BlockSpec((SMALL, GLA_KW), const),
                  pl.BlockSpec((1, GLA_KW), const),
                  pl.BlockSpec((1, GLA_VAL), const)],
        out_specs=pl.BlockSpec((1, tt, D_MODEL), lambda b, i: (b, i, 0)),
        out_shape=jax.ShapeDtypeStruct((bsz, t, D_MODEL), BF16),
        scratch_shapes=[pltpu.VMEM((tt + CARRY, 3 * DN_WIDTH), F32),
                        pltpu.VMEM((tt, DN_WIDTH), F32),
                        pltpu.VMEM((tt, DN_WIDTH), F32),
                        pltpu.VMEM((tt, DN_WIDTH), F32),
                        pltpu.VMEM((tt, 2 * SMALL), BF16),
                        pltpu.VMEM((tt, BC_W), F32),
                        pltpu.VMEM((tt, GLA_KW), F32),
                        pltpu.VMEM((n_chunks, QUAD, DN_DIM), BF16),
                        pltpu.VMEM((n_chunks, QUAD, DN_DIM), F32),
                        pltpu.VMEM((n_chunks, QUAD, QUAD), BF16),
                        pltpu.VMEM((n_chunks, CHUNK, DN_WIDTH), BF16),
                        pltpu.VMEM((n_chunks, CHUNK, DN_WIDTH), BF16),
                        pltpu.VMEM((DN_HEADS, DN_DIM, DN_DIM), F32),
                        pltpu.VMEM((GLA_VAL, GLA_KW), F32)],
        compiler_params=pltpu.CompilerParams(
            dimension_semantics=("arbitrary", "arbitrary"), vmem_limit_bytes=VMEM_LIMIT),
        name="mixer",
    )(proj, tri, sel, mask_q, mask_w, mask_r, dn_conv, alog_p, dtb_p, dn_norm_g.reshape(1, -1),
      wg2_p, b_gate.reshape(1, -1), gla_norm_g.reshape(1, -1))


def _ffn_kernel(x_ref, o_ref, mod_ref, g0_ref, b0_ref, wo_ref, g1_ref, b1_ref,
                wup_ref, cw_ref, cb_ref, wdn_ref, g2_ref, b2_ref,
                out_ref, ue_scr, *, tf, fc):
    t_idx = pl.program_id(1)

    @pl.when(t_idx == 0)
    def _():
        ue_scr[:, 0:CARRY, :] = jnp.zeros((ue_scr.shape[0], CARRY, ue_scr.shape[2]), F32)

    gt_a = mod_ref[0, :, 2 * D_MODEL:3 * D_MODEL]
    sh_f = mod_ref[0, :, 3 * D_MODEL:4 * D_MODEL]
    sc_f = mod_ref[0, :, 4 * D_MODEL:5 * D_MODEL]
    gt_f = mod_ref[0, :, 5 * D_MODEL:6 * D_MODEL]

    x0 = _layer_norm(x_ref[0], g0_ref[...], b0_ref[...])
    y = _dot(o_ref[0], wo_ref[...])
    x1 = _layer_norm(ALPHA * x0 + (1.0 + gt_a) * y, g1_ref[...], b1_ref[...])
    h2 = (x1 * (1.0 + sc_f) + sh_f).astype(BF16)

    n_chunks = D_FF // fc
    y2 = None
    for j in range(n_chunks):
        halves = []
        for part in range(2):
            c0 = part * D_FF + j * fc
            slot = 2 * j + part
            ue_scr[slot, CARRY:CARRY + tf, :] = _dot(h2, wup_ref[:, c0:c0 + fc])
            acc = cb_ref[:, c0:c0 + fc]
            for k in range(FFN_CONV):
                off = CARRY - (FFN_CONV - 1) + k
                acc = acc + ue_scr[slot, off:off + tf, :] * cw_ref[k:k + 1, c0:c0 + fc]
            ue_scr[slot, 0:CARRY, :] = ue_scr[slot, tf:tf + CARRY, :]
            halves.append(acc)
        act = (_silu(halves[0]) * halves[1]).astype(BF16)
        part_y = _dot(act, wdn_ref[j * fc:(j + 1) * fc, :])
        y2 = part_y if y2 is None else y2 + part_y
    out_ref[0] = _layer_norm(ALPHA * x1 + (1.0 + gt_f) * y2, g2_ref[...], b2_ref[...])


def _ffn(x, o, mod3, ln0_g, ln0_b, w_o, ln1_g, ln1_b, w_up, conv_w, conv_b, w_down,
         ln2_g, ln2_b, tf, fc):
    bsz, t, _ = x.shape
    const = lambda b, i: (0, 0)
    once = pl.Buffered(1)
    vec = lambda a: a.reshape(1, -1)
    return pl.pallas_call(
        functools.partial(_ffn_kernel, tf=tf, fc=fc),
        grid=(bsz, t // tf),
        in_specs=[pl.BlockSpec((1, tf, D_MODEL), lambda b, i: (b, i, 0)),
                  pl.BlockSpec((1, tf, D_MODEL), lambda b, i: (b, i, 0)),
                  pl.BlockSpec((1, 1, 6 * D_MODEL), lambda b, i: (b, 0, 0)),
                  pl.BlockSpec((1, D_MODEL), const),
                  pl.BlockSpec((1, D_MODEL), const),
                  pl.BlockSpec((D_MODEL, D_MODEL), const, pipeline_mode=once),
                  pl.BlockSpec((1, D_MODEL), const),
                  pl.BlockSpec((1, D_MODEL), const),
                  pl.BlockSpec((D_MODEL, 2 * D_FF), const, pipeline_mode=once),
                  pl.BlockSpec((FFN_CONV, 2 * D_FF), const),
                  pl.BlockSpec((1, 2 * D_FF), const),
                  pl.BlockSpec((D_FF, D_MODEL), const, pipeline_mode=once),
                  pl.BlockSpec((1, D_MODEL), const),
                  pl.BlockSpec((1, D_MODEL), const)],
        out_specs=pl.BlockSpec((1, tf, D_MODEL), lambda b, i: (b, i, 0)),
        out_shape=jax.ShapeDtypeStruct((bsz, t, D_MODEL), F32),
        scratch_shapes=[pltpu.VMEM((2 * (D_FF // fc), tf + CARRY, fc), F32)],
        compiler_params=pltpu.CompilerParams(
            dimension_semantics=("arbitrary", "arbitrary"), vmem_limit_bytes=VMEM_LIMIT),
        name="ffn",
    )(x, o, mod3, vec(ln0_g), vec(ln0_b), w_o, vec(ln1_g), vec(ln1_b), w_up, conv_w,
      vec(conv_b), w_down, vec(ln2_g), vec(ln2_b))


def _pad_lanes(a, offset, width):
    a2 = a.reshape(1, -1)
    return jnp.pad(a2, ((0, 0), (offset, width - offset - a2.shape[1])))


def kernel(x, c, ln0_g, ln0_b, w_ada, b_ada, w_in, dn_conv, dn_a_log, dn_dt_bias, dn_norm_g,
           gla_w_gate2, gla_b_gate, gla_norm_g, w_o, ln1_g, ln1_b, ffn_w_up, ffn_conv, ffn_conv_b,
           ffn_w_down, ln2_g, ln2_b):
    bsz, t, _ = x.shape
    tm, tt, tf, fc = 512, 512, 512, 1408

    wi = w_in[0]
    o_q = 0
    o_a = 4 * DN_WIDTH
    o_b = o_a + DN_HEADS
    o_gq = o_b + DN_HEADS
    o_gr = o_gq + 2 * GLA_KW + 2 * GLA_WIDTH
    small_cols = jnp.concatenate(
        [wi[:, o_a:o_a + DN_HEADS], wi[:, o_b:o_b + DN_HEADS], wi[:, o_gr:o_gr + GLA_RANK],
         jnp.zeros((D_MODEL, SMALL - 2 * DN_HEADS - GLA_RANK), wi.dtype)], axis=1)
    w_in_r = jnp.concatenate([wi[:, o_q:o_a], wi[:, o_gq:o_gr], small_cols], axis=1).astype(BF16)
    alog_p = _pad_lanes(dn_a_log[0], SM_A, SMALL)
    dtb_p = _pad_lanes(dn_dt_bias[0], SM_A, SMALL)
    wg2_p = jnp.pad(gla_w_gate2[0], ((SM_R, SMALL - SM_R - GLA_RANK), (0, 0))).astype(BF16)

    mod = _ada(c, w_ada[0], b_ada[0])
    mod3 = mod.reshape(bsz, 1, 6 * D_MODEL)
    proj = _inproj(x, mod3, ln0_g, ln0_b, w_in_r, tm)
    o = _mixer(proj, dn_conv[0], alog_p, dtb_p, dn_norm_g[0], wg2_p, gla_b_gate[0],
               gla_norm_g[0], tt)
    return _ffn(x, o, mod3, ln0_g, ln0_b, w_o[0].astype(BF16), ln1_g[0], ln1_b[0],
                ffn_w_up[0].astype(BF16), ffn_conv[0], ffn_conv_b[0],
                ffn_w_down[0].astype(BF16), ln2_g[0], ln2_b[0], tf, fc)
```

```python
import functools

import jax
import jax.numpy as jnp
import numpy as np
from jax import lax
from jax.experimental import pallas as pl
from jax.experimental.pallas import tpu as pltpu

F32 = jnp.float32
BF16 = jnp.bfloat16

D_MODEL = 1024
DN_HEADS = 4
DN_DIM = 128
DN_WIDTH = DN_HEADS * DN_DIM
SHORT_CONV = 4
GLA_HEADS = 4
GLA_KEY = 64
GLA_VAL = 128
GLA_KW = GLA_HEADS * GLA_KEY
GLA_WIDTH = GLA_HEADS * GLA_VAL
GLA_RANK = 16
GLA_TAU = 16.0
CHUNK = 64
D_FF = 2816
FFN_CONV = 3
ALPHA = 2.0 ** 0.25
EPS = 1e-6
NH = 4
QUAD = NH * CHUNK
PAIR = 2 * DN_DIM
TRI_SPAN = 256
FF_SLAB = 256

C_QKV = 0
C_Z = 3 * DN_WIDTH
C_GQ = C_Z + DN_WIDTH
C_GK = C_GQ + GLA_KW
C_GV = C_GK + GLA_KW
C_GG = C_GV + GLA_WIDTH
C_SM = C_GG + GLA_WIDTH
SMALL = 128
PROJ_W = C_SM + SMALL
SM_A, SM_B, SM_R = 0, DN_HEADS, 2 * DN_HEADS

BC_GQ = 0
BC_G = QUAD
BC_B = BC_G + DN_WIDTH
BC_W = BC_B + DN_WIDTH

CARRY = 8
VMEM_LIMIT = 56 * 1024 * 1024


def _dot(a, b):
    return jnp.dot(a, b, preferred_element_type=F32)


def _dot_nt(a, b):
    return lax.dot_general(a, b, (((1,), (1,)), ((), ())), preferred_element_type=F32)


def _dot_tn(a, b):
    return lax.dot_general(a, b, (((0,), (0,)), ((), ())), preferred_element_type=F32)


def _bdot(a, b):
    return _dot(a.astype(BF16), b.astype(BF16))


def _split(a):
    hi = a.astype(BF16)
    lo = (a - hi.astype(F32)).astype(BF16)
    return hi, lo


def _dot_exact_lhs(a_bf16, b):
    bh, bl = _split(b)
    return _dot(a_bf16, bh) + _dot(a_bf16, bl)


def _silu(x):
    h = 0.5 * x
    return h + h * jnp.tanh(h)


def _shifted_taps(x_ext, n_taps, rows):
    taps = []
    for k in range(n_taps):
        shift = n_taps - 1 - k
        rolled = x_ext if shift == 0 else pltpu.roll(x_ext, shift, 0)
        taps.append(rolled[CARRY:CARRY + rows])
    return taps


def _layer_norm(x, g, b):
    mu = jnp.mean(x, axis=-1, keepdims=True)
    xc = x - mu
    var = jnp.mean(xc * xc, axis=-1, keepdims=True)
    return xc * lax.rsqrt(var + EPS) * g + b


def _rows_by_head(a, width):
    return jnp.concatenate([a[:, h * width:(h + 1) * width] for h in range(NH)], axis=0)


def _stack4(a):
    return jnp.concatenate([a, a, a, a], axis=0)


def _ada_kernel(c_ref, w_ref, b_ref, o_ref):
    cond = _silu(c_ref[...])
    o_ref[...] = _bdot(cond, w_ref[...]) + b_ref[...]


def _ada(c, w_ada, b_ada):
    bsz = c.shape[0]
    n_out = w_ada.shape[1]
    blk = D_MODEL
    return pl.pallas_call(
        _ada_kernel,
        grid=(n_out // blk,),
        in_specs=[pl.BlockSpec((bsz, D_MODEL), lambda j: (0, 0)),
                  pl.BlockSpec((D_MODEL, blk), lambda j: (0, j)),
                  pl.BlockSpec((1, blk), lambda j: (0, j))],
        out_specs=pl.BlockSpec((bsz, blk), lambda j: (0, j)),
        out_shape=jax.ShapeDtypeStruct((bsz, n_out), F32),
        name="ada_mod",
    )(c, w_ada, b_ada.reshape(1, n_out))


def _inproj_kernel(x_ref, mod_ref, g_ref, b_ref, w_ref, o_ref):
    x0 = _layer_norm(x_ref[0], g_ref[...], b_ref[...])
    sh = mod_ref[0, :, 0:D_MODEL]
    sc = mod_ref[0, :, D_MODEL:2 * D_MODEL]
    h = x0 * (1.0 + sc) + sh
    o_ref[0] = _dot(h.astype(BF16), w_ref[...])


def _inproj(x, mod3, ln0_g, ln0_b, w_in_r, tm):
    bsz, t, _ = x.shape
    const = lambda b, i: (0, 0)
    return pl.pallas_call(
        _inproj_kernel,
        grid=(bsz, t // tm),
        in_specs=[pl.BlockSpec((1, tm, D_MODEL), lambda b, i: (b, i, 0)),
                  pl.BlockSpec((1, 1, 6 * D_MODEL), lambda b, i: (b, 0, 0)),
                  pl.BlockSpec((1, D_MODEL), const),
                  pl.BlockSpec((1, D_MODEL), const),
                  pl.BlockSpec((D_MODEL, PROJ_W), const)],
        out_specs=pl.BlockSpec((1, tm, PROJ_W), lambda b, i: (b, i, 0)),
        out_shape=jax.ShapeDtypeStruct((bsz, t, PROJ_W), F32),
        compiler_params=pltpu.CompilerParams(
            dimension_semantics=("arbitrary", "arbitrary"), vmem_limit_bytes=VMEM_LIMIT),
        name="inproj",
    )(x, mod3, ln0_g.reshape(1, -1), ln0_b.reshape(1, -1), w_in_r)


def _mixer_kernel(proj_ref, tri_ref, sel_ref, mq_ref, mw_ref, mr_ref, cw_ref, alog_ref, dtb_ref,
                  dng_ref, wg2_ref, bg_ref, glg_ref,
                  o_ref,
                  xe_scr, qn_scr, kn_scr, v_scr, bc_scr, gb_scr, dec_scr,
                  pw_scr, q_scr, z_scr, sdn_scr, sgl_scr, *, tt, group):
    t_idx = pl.program_id(1)
    n_chunks = tt // CHUNK

    @pl.when(t_idx == 0)
    def _():
        xe_scr[0:CARRY, :] = jnp.zeros((CARRY, 3 * DN_WIDTH), F32)
        sdn_scr[...] = jnp.zeros_like(sdn_scr)
        sgl_scr[...] = jnp.zeros_like(sgl_scr)

    xe_scr[CARRY:CARRY + tt, :] = proj_ref[0, :, C_QKV:C_QKV + 3 * DN_WIDTH]
    for j in range(3 * DN_HEADS):
        cs = slice(j * DN_DIM, (j + 1) * DN_DIM)
        conv = None
        for k, tap in enumerate(_shifted_taps(xe_scr[:, cs], SHORT_CONV, tt)):
            term = tap * cw_ref[k:k + 1, cs]
            conv = term if conv is None else conv + term
        act = _silu(conv)
        hs = slice((j % DN_HEADS) * DN_DIM, (j % DN_HEADS + 1) * DN_DIM)
        if j < DN_HEADS:
            qn_scr[:, hs] = act * (lax.rsqrt(jnp.sum(act * act, -1, keepdims=True) + EPS)
                                   * (DN_DIM ** -0.5))
        elif j < 2 * DN_HEADS:
            kn_scr[:, hs] = act * lax.rsqrt(jnp.sum(act * act, -1, keepdims=True) + EPS)
        else:
            v_scr[:, hs] = act
    xe_scr[0:CARRY, :] = xe_scr[tt:tt + CARRY, :]

    small = proj_ref[0, :, C_SM:C_SM + SMALL]
    lane = lax.broadcasted_iota(jnp.int32, (tt, SMALL), 1)
    log_a = -jnp.exp(alog_ref[...]) * jax.nn.softplus(small + dtb_ref[...])
    log_a = jnp.where(lane < DN_HEADS, log_a, 0.0)
    def chunk_cumsum(a):
        span = tri_ref.shape[0]
        return jnp.concatenate(
            [_dot_exact_lhs(tri_ref[...], a[r:r + span]) for r in range(0, tt, span)], axis=0)

    def quad_masks():
        row = lax.broadcasted_iota(jnp.int32, (CHUNK, QUAD), 0)
        col = lax.broadcasted_iota(jnp.int32, (CHUNK, QUAD), 1) & (CHUNK - 1)
        return row, col

    g_cum = chunk_cumsum(log_a)
    sc = jnp.where(lane < DN_HEADS, g_cum, jax.nn.sigmoid(small))
    sc_hi, sc_lo = _split(sc)
    sl = jnp.concatenate([sc_hi, sc_lo], axis=1)
    bc_scr[...] = _dot(sl, sel_ref[...])

    row_q, col_q = quad_masks()
    ones_lhs = jnp.ones((CHUNK, 2 * SMALL), BF16)
    for c in range(n_chunks):
        rows = slice(c * CHUNK, (c + 1) * CHUNK)
        g_row = _dot_nt(ones_lhs, _stack4(sl[rows]) * mr_ref[...])
        g_col = bc_scr[rows, BC_GQ:BC_GQ + QUAD]
        dec_scr[c] = jnp.exp(jnp.where(row_q >= col_q, g_col - g_row, -jnp.inf))

    gate_pre = _bdot(small, wg2_ref[...]) + bg_ref[...]
    log_alpha = jax.nn.log_sigmoid(gate_pre) * (1.0 / GLA_TAU)
    gb_scr[...] = chunk_cumsum(log_alpha)

    def rows_of(c):
        return pl.ds(pl.multiple_of(c * CHUNK, CHUNK), CHUNK)

    def block_diag(a_bf16):
        return _stack4(a_bf16) * mq_ref[...]

    def dn_prepare(c):
        rows = rows_of(c)
        row, col = quad_masks()
        kn = kn_scr[rows, :]
        qn = qn_scr[rows, :]
        vv = v_scr[rows, :]
        g_w = bc_scr[rows, BC_G:BC_G + DN_WIDTH]
        b_w = bc_scr[rows, BC_B:BC_B + DN_WIDTH]
        decay = dec_scr[c]
        kb = kn * b_w
        k_bd = _stack4(kn.astype(BF16)) * mw_ref[...]
        aqk = _dot_nt(jnp.concatenate([kb, qn], axis=0).astype(BF16), k_bd)
        yield
        m_low = jnp.where(row > col, aqk[0:CHUNK] * decay, 0.0)
        attn = aqk[CHUNK:2 * CHUNK] * decay
        p = -m_low
        t = jnp.where(row == col, 1.0, 0.0) + p
        n_levels = CHUNK.bit_length() - 1
        for lvl in range(n_levels):
            bd = block_diag(p.astype(BF16))
            if lvl == 0:
                p = _dot(p.astype(BF16), bd)
                yield
            elif lvl < n_levels - 1:
                both = _dot(jnp.concatenate([p, t], axis=0).astype(BF16), bd)
                yield
                p = both[0:CHUNK]
                t = t + both[CHUNK:2 * CHUNK]
            else:
                t_inc = _dot(t.astype(BF16), bd)
                yield
                t = t + t_inc
        eg = jnp.exp(g_w)
        kg = (kb * eg).astype(BF16)
        vb = (vv * b_w).astype(BF16)
        rhs = jnp.concatenate(
            [jnp.concatenate([kg[:, h * DN_DIM:(h + 1) * DN_DIM],
                              vb[:, h * DN_DIM:(h + 1) * DN_DIM]], axis=1) for h in range(NH)],
            axis=0)
        wu = _dot(block_diag(t.astype(BF16)), rhs)
        g_last = g_w[CHUNK - 1:CHUNK, :]
        kd = (kn * jnp.exp(g_last - g_w)).astype(BF16)
        yield
        wu = wu.astype(BF16)
        aw = _dot(block_diag(attn.astype(BF16)), wu)
        pq = [_dot_tn(kd[:, h * DN_DIM:(h + 1) * DN_DIM], wu[h * CHUNK:(h + 1) * CHUNK, :])
              for h in range(NH)]
        yield
        z_scr[c] = jnp.concatenate(
            [aw[h * CHUNK:(h + 1) * CHUNK, DN_DIM:2 * DN_DIM] for h in range(NH)], axis=1)
        q_eff = qn * eg - jnp.concatenate(
            [aw[h * CHUNK:(h + 1) * CHUNK, 0:DN_DIM] for h in range(NH)], axis=1)
        for pr in range(NH // 2):
            h0, h1 = 2 * pr, 2 * pr + 1
            pw_scr[c, pr, 0:DN_DIM, :] = jnp.concatenate(
                [pq[h0][:, 0:DN_DIM], pq[h1][:, 0:DN_DIM]], axis=1).astype(BF16)
            pw_scr[c, pr, DN_DIM:DN_DIM + CHUNK, :] = q_eff[:, pr * PAIR:(pr + 1) * PAIR].astype(BF16)
            q_scr[c, pr] = jnp.concatenate(
                [pq[h0][:, DN_DIM:2 * DN_DIM], pq[h1][:, DN_DIM:2 * DN_DIM]], axis=1)

    def dn_apply(c):
        rows = rows_of(c)
        last8 = pl.ds(pl.multiple_of(c * CHUNK + CHUNK - CARRY, CARRY), CARRY)
        g_last = bc_scr[last8, BC_G:BC_G + DN_WIDTH][CARRY - 1:CARRY, :]
        eg_last = jnp.exp(g_last)
        zero = jnp.zeros((DN_DIM, DN_DIM), BF16)
        outs = []
        for pr in range(NH // 2):
            s = sdn_scr[pr]
            s_b = s.astype(BF16)
            rhs = jnp.concatenate(
                [jnp.concatenate([s_b[:, 0:DN_DIM], zero], axis=1),
                 jnp.concatenate([zero, s_b[:, DN_DIM:PAIR]], axis=1)], axis=0)
            r = _dot(pw_scr[c, pr], rhs)
            sdn_scr[pr] = (s * eg_last[:, pr * PAIR:(pr + 1) * PAIR] - r[0:DN_DIM]) + q_scr[c, pr]
            outs.append(r[DN_DIM:DN_DIM + CHUNK])
        o = jnp.concatenate(outs, axis=1) + z_scr[c]
        z = proj_ref[0, rows, C_Z:C_Z + DN_WIDTH]
        normed = []
        for h in range(NH):
            oh = o[:, h * DN_DIM:(h + 1) * DN_DIM]
            normed.append(oh * lax.rsqrt(jnp.mean(oh * oh, -1, keepdims=True) + EPS))
        on = jnp.concatenate(normed, axis=1) * dng_ref[...]
        o_ref[0, rows, 0:DN_WIDTH] = (on * _silu(z)).astype(o_ref.dtype)

    def gla_chunk(c):
        rows = rows_of(c)
        row, col = quad_masks()
        b = gb_scr[rows, :]
        q = proj_ref[0, rows, C_GQ:C_GQ + GLA_KW] * (GLA_KEY ** -0.5)
        k = proj_ref[0, rows, C_GK:C_GK + GLA_KW]
        v = proj_ref[0, rows, C_GV:C_GV + GLA_WIDTH].astype(BF16)
        b_last = b[CHUNK - 1:CHUNK, :]
        q_dec = (q * jnp.exp(b)).astype(BF16)
        k_inv = (k * jnp.exp(-b)).astype(BF16)
        k_dec = (k * jnp.exp(b_last - b)).astype(BF16)
        attn = jnp.where(row >= col, _dot_nt(q_dec, block_diag(k_inv)), 0.0)
        st = sgl_scr[...]
        v_rows = _rows_by_head(v, GLA_VAL)
        o = (_dot_nt(block_diag(q_dec), st.astype(BF16))
             + _dot(block_diag(attn.astype(BF16)), v_rows))
        sgl_scr[...] = st * jnp.exp(b_last) + _dot_tn(v_rows, block_diag(k_dec))
        gg = _rows_by_head(proj_ref[0, rows, C_GG:C_GG + GLA_WIDTH], GLA_VAL)
        on = o * lax.rsqrt(jnp.mean(o * o, -1, keepdims=True) + EPS) * glg_ref[...]
        res = (on * _silu(gg)).astype(o_ref.dtype)
        for h in range(NH):
            o_ref[0, rows, DN_WIDTH + h * GLA_VAL:DN_WIDTH + (h + 1) * GLA_VAL] = (
                res[h * CHUNK:(h + 1) * CHUNK, :])

    def prepare_group(g, carry):
        chains = [dn_prepare(g * group + u) for u in range(group)]
        live = True
        while live:
            live = False
            for chain in chains:
                live = next(chain, "done") != "done" or live
        return carry

    lax.fori_loop(0, n_chunks // group, prepare_group, 0)

    def apply_chunk(c, carry):
        dn_apply(c)
        gla_chunk(c)
        return carry

    lax.fori_loop(0, n_chunks, apply_chunk, 0)


def _mixer_constants():
    idx = np.arange(TRI_SPAN)
    tri = (idx[:, None] >= idx[None, :]) & (idx[:, None] // CHUNK == idx[None, :] // CHUNK)
    sel = np.zeros((2 * SMALL, BC_W), np.float32)
    for part in range(2):
        for h in range(NH):
            sel[part * SMALL + SM_A + h, BC_GQ + h * CHUNK:BC_GQ + (h + 1) * CHUNK] = 1.0
            sel[part * SMALL + SM_A + h, BC_G + h * DN_DIM:BC_G + (h + 1) * DN_DIM] = 1.0
            sel[part * SMALL + SM_B + h, BC_B + h * DN_DIM:BC_B + (h + 1) * DN_DIM] = 1.0
    r = np.arange(QUAD)
    mask_q = (r[:, None] // CHUNK == r[None, :] // CHUNK)
    mask_w = (r[:, None] // CHUNK == np.arange(DN_WIDTH)[None, :] // DN_DIM)
    l2 = np.arange(2 * SMALL)
    mask_r = (l2[None, :] % SMALL == SM_A + r[:, None] // CHUNK)
    as_bf16 = lambda a: jnp.asarray(a.astype(np.float32), dtype=BF16)
    return as_bf16(tri), as_bf16(sel), as_bf16(mask_q), as_bf16(mask_w), as_bf16(mask_r)


def _mixer(proj, dn_conv, alog_p, dtb_p, dn_norm_g, wg2_p, b_gate, gla_norm_g, tt, group):
    bsz, t, _ = proj.shape
    n_chunks = tt // CHUNK
    assert tt % TRI_SPAN == 0 and n_chunks % group == 0
    const = lambda b, i: (0, 0)
    tri, sel, mask_q, mask_w, mask_r = _mixer_constants()
    return pl.pallas_call(
        functools.partial(_mixer_kernel, tt=tt, group=group),
        grid=(bsz, t // tt),
        in_specs=[pl.BlockSpec((1, tt, PROJ_W), lambda b, i: (b, i, 0)),
                  pl.BlockSpec((TRI_SPAN, TRI_SPAN), const),
                  pl.BlockSpec((2 * SMALL, BC_W), const),
                  pl.BlockSpec((QUAD, QUAD), const),
                  pl.BlockSpec((QUAD, DN_WIDTH), const),
                  pl.BlockSpec((QUAD, 2 * SMALL), const),
                  pl.BlockSpec((SHORT_CONV, 3 * DN_WIDTH), const),
                  pl.BlockSpec((1, SMALL), const),
                  pl.BlockSpec((1, SMALL), const),
                  pl.BlockSpec((1, DN_WIDTH), const),
                  pl.BlockSpec((SMALL, GLA_KW), const),
                  pl.BlockSpec((1, GLA_KW), const),
                  pl.BlockSpec((1, GLA_VAL), const)],
        out_specs=pl.BlockSpec((1, tt, D_MODEL), lambda b, i: (b, i, 0)),
        out_shape=jax.ShapeDtypeStruct((bsz, t, D_MODEL), BF16),
        scratch_shapes=[pltpu.VMEM((tt + CARRY, 3 * DN_WIDTH), F32),
                        pltpu.VMEM((tt, DN_WIDTH), F32),
                        pltpu.VMEM((tt, DN_WIDTH), F32),
                        pltpu.VMEM((tt, DN_WIDTH), F32),
                        pltpu.VMEM((tt, BC_W), F32),
                        pltpu.VMEM((tt, GLA_KW), F32),
                        pltpu.VMEM((n_chunks, CHUNK, QUAD), F32),
                        pltpu.VMEM((n_chunks, NH // 2, DN_DIM + CHUNK, PAIR), BF16),
                        pltpu.VMEM((n_chunks, NH // 2, DN_DIM, PAIR), F32),
                        pltpu.VMEM((n_chunks, CHUNK, DN_WIDTH), F32),
                        pltpu.VMEM((NH // 2, DN_DIM, PAIR), F32),
                        pltpu.VMEM((GLA_VAL, GLA_KW), F32)],
        compiler_params=pltpu.CompilerParams(
            dimension_semantics=("arbitrary", "arbitrary"), vmem_limit_bytes=VMEM_LIMIT),
        name="mixer",
    )(proj, tri, sel, mask_q, mask_w, mask_r, dn_conv, alog_p, dtb_p,
      jnp.tile(dn_norm_g.reshape(1, -1), (1, NH)), wg2_p, b_gate.reshape(1, -1),
      gla_norm_g.reshape(1, -1))


def _ffn_kernel(x_ref, o_ref, mod_ref, g0_ref, b0_ref, wo_ref, g1_ref, b1_ref,
                wup_ref, cw_ref, cb_ref, wdn_ref, g2_ref, b2_ref,
                out_ref, ue_scr, act_scr, *, tf, ff_chunks):
    t_idx = pl.program_id(1)

    @pl.when(t_idx == 0)
    def _():
        ue_scr[:, 0:CARRY, :] = jnp.zeros((ue_scr.shape[0], CARRY, ue_scr.shape[2]), F32)

    gt_a = mod_ref[0, :, 2 * D_MODEL:3 * D_MODEL]
    sh_f = mod_ref[0, :, 3 * D_MODEL:4 * D_MODEL]
    sc_f = mod_ref[0, :, 4 * D_MODEL:5 * D_MODEL]
    gt_f = mod_ref[0, :, 5 * D_MODEL:6 * D_MODEL]

    x0 = _layer_norm(x_ref[0], g0_ref[...], b0_ref[...])
    y = _dot(o_ref[0], wo_ref[...])
    x1 = _layer_norm(ALPHA * x0 + (1.0 + gt_a) * y, g1_ref[...], b1_ref[...])
    h2 = (x1 * (1.0 + sc_f) + sh_f).astype(BF16)

    y2 = None
    for j, (f0, fw) in enumerate(ff_chunks):
        for part in range(2):
            c0 = part * D_FF + f0
            ue_scr[2 * j + part, CARRY:CARRY + tf, 0:fw] = _dot(h2, wup_ref[:, c0:c0 + fw])
        for s0 in range(0, fw, FF_SLAB):
            halves = []
            for part in range(2):
                c0 = part * D_FF + f0 + s0
                slot = 2 * j + part
                acc = cb_ref[:, c0:c0 + FF_SLAB]
                taps = _shifted_taps(ue_scr[slot, :, s0:s0 + FF_SLAB], FFN_CONV, tf)
                for k, tap in enumerate(taps):
                    acc = acc + tap * cw_ref[k:k + 1, c0:c0 + FF_SLAB]
                halves.append(acc)
            act_scr[:, s0:s0 + FF_SLAB] = (_silu(halves[0]) * halves[1]).astype(BF16)
        for part in range(2):
            slot = 2 * j + part
            ue_scr[slot, 0:CARRY, 0:fw] = ue_scr[slot, tf:tf + CARRY, 0:fw]
        part_y = _dot(act_scr[:, 0:fw], wdn_ref[f0:f0 + fw, :])
        y2 = part_y if y2 is None else y2 + part_y
    out_ref[0] = _layer_norm(ALPHA * x1 + (1.0 + gt_f) * y2, g2_ref[...], b2_ref[...])


def _ffn(x, o, mod3, ln0_g, ln0_b, w_o, ln1_g, ln1_b, w_up, conv_w, conv_b, w_down,
         ln2_g, ln2_b, tf, ff_chunks):
    bsz, t, _ = x.shape
    const = lambda b, i: (0, 0)
    once = pl.Buffered(1)
    vec = lambda a: a.reshape(1, -1)
    fw_max = max(fw for _, fw in ff_chunks)
    assert sum(fw for _, fw in ff_chunks) == D_FF and all(fw % FF_SLAB == 0 for _, fw in ff_chunks)
    return pl.pallas_call(
        functools.partial(_ffn_kernel, tf=tf, ff_chunks=ff_chunks),
        grid=(bsz, t // tf),
        in_specs=[pl.BlockSpec((1, tf, D_MODEL), lambda b, i: (b, i, 0)),
                  pl.BlockSpec((1, tf, D_MODEL), lambda b, i: (b, i, 0)),
                  pl.BlockSpec((1, 1, 6 * D_MODEL), lambda b, i: (b, 0, 0)),
                  pl.BlockSpec((1, D_MODEL), const),
                  pl.BlockSpec((1, D_MODEL), const),
                  pl.BlockSpec((D_MODEL, D_MODEL), const, pipeline_mode=once),
                  pl.BlockSpec((1, D_MODEL), const),
                  pl.BlockSpec((1, D_MODEL), const),
                  pl.BlockSpec((D_MODEL, 2 * D_FF), const, pipeline_mode=once),
                  pl.BlockSpec((FFN_CONV, 2 * D_FF), const),
                  pl.BlockSpec((1, 2 * D_FF), const),
                  pl.BlockSpec((D_FF, D_MODEL), const, pipeline_mode=once),
                  pl.BlockSpec((1, D_MODEL), const),
                  pl.BlockSpec((1, D_MODEL), const)],
        out_specs=pl.BlockSpec((1, tf, D_MODEL), lambda b, i: (b, i, 0)),
        out_shape=jax.ShapeDtypeStruct((bsz, t, D_MODEL), F32),
        scratch_shapes=[pltpu.VMEM((2 * len(ff_chunks), tf + CARRY, fw_max), F32),
                        pltpu.VMEM((tf, fw_max), BF16)],
        compiler_params=pltpu.CompilerParams(
            dimension_semantics=("arbitrary", "arbitrary"), vmem_limit_bytes=VMEM_LIMIT),
        name="ffn",
    )(x, o, mod3, vec(ln0_g), vec(ln0_b), w_o, vec(ln1_g), vec(ln1_b), w_up, conv_w,
      vec(conv_b), w_down, vec(ln2_g), vec(ln2_b))


def _pad_lanes(a, offset, width):
    a2 = a.reshape(1, -1)
    return jnp.pad(a2, ((0, 0), (offset, width - offset - a2.shape[1])))


def kernel(x, c, ln0_g, ln0_b, w_ada, b_ada, w_in, dn_conv, dn_a_log, dn_dt_bias, dn_norm_g,
           gla_w_gate2, gla_b_gate, gla_norm_g, w_o, ln1_g, ln1_b, ffn_w_up, ffn_conv, ffn_conv_b,
           ffn_w_down, ln2_g, ln2_b):
    bsz, t, _ = x.shape
    tm, tt, tf = 512, 512, 512
    group = 8
    ff_chunks = ((0, 1536), (1536, 1280))

    wi = w_in[0]
    o_q = 0
    o_a = 4 * DN_WIDTH
    o_b = o_a + DN_HEADS
    o_gq = o_b + DN_HEADS
    o_gr = o_gq + 2 * GLA_KW + 2 * GLA_WIDTH
    small_cols = jnp.concatenate(
        [wi[:, o_a:o_a + DN_HEADS], wi[:, o_b:o_b + DN_HEADS], wi[:, o_gr:o_gr + GLA_RANK],
         jnp.zeros((D_MODEL, SMALL - 2 * DN_HEADS - GLA_RANK), wi.dtype)], axis=1)
    w_in_r = jnp.concatenate([wi[:, o_q:o_a], wi[:, o_gq:o_gr], small_cols], axis=1).astype(BF16)
    alog_p = _pad_lanes(dn_a_log[0], SM_A, SMALL)
    dtb_p = _pad_lanes(dn_dt_bias[0], SM_A, SMALL)
    wg2_p = jnp.pad(gla_w_gate2[0], ((SM_R, SMALL - SM_R - GLA_RANK), (0, 0))).astype(BF16)

    mod = _ada(c, w_ada[0], b_ada[0])
    mod3 = mod.reshape(bsz, 1, 6 * D_MODEL)
    proj = _inproj(x, mod3, ln0_g, ln0_b, w_in_r, tm)
    o = _mixer(proj, dn_conv[0], alog_p, dtb_p, dn_norm_g[0], wg2_p, gla_b_gate[0],
               gla_norm_g[0], tt, group)
    return _ffn(x, o, mod3, ln0_g, ln0_b, w_o[0].astype(BF16), ln1_g[0], ln1_b[0],
                ffn_w_up[0].astype(BF16), ffn_conv[0], ffn_conv_b[0],
                ffn_w_down[0].astype(BF16), ln2_g[0], ln2_b[0], tf, ff_chunks)
```

```python
import functools

import jax
import jax.numpy as jnp
import numpy as np
from jax import lax
from jax.experimental import pallas as pl
from jax.experimental.pallas import tpu as pltpu

F32 = jnp.float32
BF16 = jnp.bfloat16

D_MODEL = 1024
DN_HEADS = 4
DN_DIM = 128
DN_WIDTH = DN_HEADS * DN_DIM
SHORT_CONV = 4
GLA_HEADS = 4
GLA_KEY = 64
GLA_VAL = 128
GLA_KW = GLA_HEADS * GLA_KEY
GLA_WIDTH = GLA_HEADS * GLA_VAL
GLA_RANK = 16
GLA_TAU = 16.0
CHUNK = 64
D_FF = 2816
FFN_CONV = 3
ALPHA = 2.0 ** 0.25
EPS = 1e-6
NH = 4
QUAD = NH * CHUNK
PAIR = 2 * DN_DIM
TRI_SPAN = 256
FF_SLAB = 256
C_QKV = 0
C_Z = 3 * DN_WIDTH
C_GQ = C_Z + DN_WIDTH
C_GK = C_GQ + GLA_KW
C_GV = C_GK + GLA_KW
C_GG = C_GV + GLA_WIDTH
C_SM = C_GG + GLA_WIDTH
SMALL = 128
PROJ_W = C_SM + SMALL
SM_A, SM_B, SM_R = 0, DN_HEADS, 2 * DN_HEADS
P_Z, P_GQ, P_GK, P_GV, P_GG = 0, C_GQ - C_Z, C_GK - C_Z, C_GV - C_Z, C_GG - C_Z

BC_GQ = 0
BC_G = QUAD
BC_B = BC_G + DN_WIDTH
BC_W = BC_B + DN_WIDTH

CARRY = 8
VMEM_LIMIT = 56 * 1024 * 1024


def _dot(a, b):
    return jnp.dot(a, b, preferred_element_type=F32)


def _dot_nt(a, b):
    return lax.dot_general(a, b, (((1,), (1,)), ((), ())), preferred_element_type=F32)


def _dot_tn(a, b):
    return lax.dot_general(a, b, (((0,), (0,)), ((), ())), preferred_element_type=F32)


def _bdot(a, b):
    return _dot(a.astype(BF16), b.astype(BF16))


def _split(a):
    hi = a.astype(BF16)
    lo = (a - hi.astype(F32)).astype(BF16)
    return hi, lo


def _dot_exact_lhs(a_bf16, b):
    bh, bl = _split(b)
    return _dot(a_bf16, bh) + _dot(a_bf16, bl)


def _silu(x):
    h = 0.5 * x
    return h + h * jnp.tanh(h)


def _shifted_taps(x_ext, n_taps, rows):
    taps = []
    for k in range(n_taps):
        shift = n_taps - 1 - k
        rolled = x_ext if shift == 0 else pltpu.roll(x_ext, shift, 0)
        taps.append(rolled[CARRY:CARRY + rows])
    return taps


def _layer_norm(x, g, b):
    mu = jnp.mean(x, axis=-1, keepdims=True)
    xc = x - mu
    var = jnp.mean(xc * xc, axis=-1, keepdims=True)
    return xc * lax.rsqrt(var + EPS) * g + b


def _rows_by_head(a, width):
    return jnp.concatenate([a[:, h * width:(h + 1) * width] for h in range(NH)], axis=0)


def _stack4(a):
    return jnp.concatenate([a, a, a, a], axis=0)


def _ada_kernel(c_ref, w_ref, b_ref, o_ref):
    cond = _silu(c_ref[...])
    o_ref[...] = _bdot(cond, w_ref[...]) + b_ref[...]


def _ada(c, w_ada, b_ada):
    bsz = c.shape[0]
    n_out = w_ada.shape[1]
    blk = D_MODEL
    return pl.pallas_call(
        _ada_kernel,
        grid=(n_out // blk,),
        in_specs=[pl.BlockSpec((bsz, D_MODEL), lambda j: (0, 0)),
                  pl.BlockSpec((D_MODEL, blk), lambda j: (0, j)),
                  pl.BlockSpec((1, blk), lambda j: (0, j))],
        out_specs=pl.BlockSpec((bsz, blk), lambda j: (0, j)),
        out_shape=jax.ShapeDtypeStruct((bsz, n_out), F32),
        name="ada_mod",
    )(c, w_ada, b_ada.reshape(1, n_out))


_DONE = object()


def _drive(tasks):
    tasks = list(tasks)
    while tasks:
        tasks = [task for task in tasks if next(task, _DONE) is not _DONE]


def _mixer_kernel(x_ref, mod_ref, g0_ref, b0_ref, win_ref, tri_ref, sel_ref, mq_ref, mr_ref,
                  cw_ref, alog_ref, dtb_ref, dng_ref, wg2_ref, bg_ref, glg_ref,
                  o_ref,
                  h_scr, proj_scr, xe_scr, qn_scr, kn_scr, v_scr, bc_scr, gb_scr, dec_scr,
                  pw_scr, q_scr, z_scr, sdn_scr, sgl_scr, *, tt):
    t_idx = pl.program_id(1)
    n_chunks = tt // CHUNK

    @pl.when(t_idx == 0)
    def _():
        xe_scr[0:CARRY, :] = jnp.zeros((CARRY, 3 * DN_WIDTH), F32)
        sdn_scr[...] = jnp.zeros_like(sdn_scr)
        sgl_scr[...] = jnp.zeros_like(sgl_scr)

    x0 = _layer_norm(x_ref[0], g0_ref[...], b0_ref[...])
    h_scr[...] = (x0 * (1.0 + mod_ref[0, :, D_MODEL:2 * D_MODEL])
                  + mod_ref[0, :, 0:D_MODEL]).astype(BF16)

    def project(c0, width):
        return _dot(h_scr[...], win_ref[:, c0:c0 + width])

    small = project(C_SM, SMALL)

    for j in range(3 * DN_HEADS):
        if j % DN_HEADS == 0:
            xe_scr[CARRY:CARRY + tt, j * DN_DIM:j * DN_DIM + DN_WIDTH] = project(
                C_QKV + j * DN_DIM, DN_WIDTH)
        cs = slice(j * DN_DIM, (j + 1) * DN_DIM)
        conv = None
        for k, tap in enumerate(_shifted_taps(xe_scr[:, cs], SHORT_CONV, tt)):
            term = tap * cw_ref[k:k + 1, cs]
            conv = term if conv is None else conv + term
        act = _silu(conv)
        hs = slice((j % DN_HEADS) * DN_DIM, (j % DN_HEADS + 1) * DN_DIM)
        if j < DN_HEADS:
            qn_scr[:, hs] = act * (lax.rsqrt(jnp.sum(act * act, -1, keepdims=True) + EPS)
                                   * (DN_DIM ** -0.5))
        elif j < 2 * DN_HEADS:
            kn_scr[:, hs] = act * lax.rsqrt(jnp.sum(act * act, -1, keepdims=True) + EPS)
        else:
            v_scr[:, hs] = act
    xe_scr[0:CARRY, :] = xe_scr[tt:tt + CARRY, :]
    proj_scr[...] = project(C_Z, C_SM - C_Z)

    lane = lax.broadcasted_iota(jnp.int32, (tt, SMALL), 1)
    log_a = -jnp.exp(alog_ref[...]) * jax.nn.softplus(small + dtb_ref[...])
    log_a = jnp.where(lane < DN_HEADS, log_a, 0.0)
    def chunk_cumsum(a):
        span = tri_ref.shape[0]
        return jnp.concatenate(
            [_dot_exact_lhs(tri_ref[...], a[r:r + span]) for r in range(0, tt, span)], axis=0)

    def quad_masks():
        row = lax.broadcasted_iota(jnp.int32, (CHUNK, QUAD), 0)
        col = lax.broadcasted_iota(jnp.int32, (CHUNK, QUAD), 1) & (CHUNK - 1)
        return row, col

    g_cum = chunk_cumsum(log_a)
    sc = jnp.where(lane < DN_HEADS, g_cum, jax.nn.sigmoid(small))
    sc_hi, sc_lo = _split(sc)
    sl = jnp.concatenate([sc_hi, sc_lo], axis=1)
    bc_scr[...] = _dot(sl, sel_ref[...])

    row_q, col_q = quad_masks()
    ones_lhs = jnp.ones((CHUNK, 2 * SMALL), BF16)
    for c in range(n_chunks):
        rows = slice(c * CHUNK, (c + 1) * CHUNK)
        g_row = _dot_nt(ones_lhs, _stack4(sl[rows]) * mr_ref[...])
        g_col = bc_scr[rows, BC_GQ:BC_GQ + QUAD]
        dec_scr[c] = jnp.exp(jnp.where(row_q >= col_q, g_col - g_row, -jnp.inf))

    gate_pre = _bdot(small, wg2_ref[...]) + bg_ref[...]
    log_alpha = jax.nn.log_sigmoid(gate_pre) * (1.0 / GLA_TAU)
    gb_scr[...] = chunk_cumsum(log_alpha)

    def rows_of(c):
        return slice(c * CHUNK, (c + 1) * CHUNK)

    def block_diag(a_bf16):
        return _stack4(a_bf16) * mq_ref[...]

    def dn_prepare(c):
        rows = rows_of(c)
        row, col = quad_masks()
        kn = kn_scr[rows, :]
        qn = qn_scr[rows, :]
        vv = v_scr[rows, :]
        g_w = bc_scr[rows, BC_G:BC_G + DN_WIDTH]
        b_w = bc_scr[rows, BC_B:BC_B + DN_WIDTH]
        decay = dec_scr[c]
        kb = kn * b_w
        kn_b = kn.astype(BF16)
        zero = jnp.zeros((CHUNK, DN_DIM), BF16)
        k_bd = jnp.concatenate(
            [jnp.concatenate([kn_b[:, h * DN_DIM:(h + 1) * DN_DIM] if g == h else zero
                              for g in range(NH)], axis=1) for h in range(NH)], axis=0)
        aqk = _dot_nt(jnp.concatenate([kb, qn], axis=0).astype(BF16), k_bd)
        yield
        m_low = jnp.where(row > col, aqk[0:CHUNK] * decay, 0.0)
        attn = aqk[CHUNK:2 * CHUNK] * decay
        p = -m_low
        t = jnp.where(row == col, 1.0, 0.0) + p
        n_levels = CHUNK.bit_length() - 1
        for lvl in range(n_levels):
            bd = block_diag(p.astype(BF16))
            if lvl == 0:
                p = _dot(p.astype(BF16), bd)
                yield
            elif lvl < n_levels - 1:
                both = _dot(jnp.concatenate([p, t], axis=0).astype(BF16), bd)
                yield
                p = both[0:CHUNK]
                t = t + both[CHUNK:2 * CHUNK]
            else:
                t_inc = _dot(t.astype(BF16), bd)
                yield
                t = t + t_inc
        eg = jnp.exp(g_w)
        kg = (kb * eg).astype(BF16)
        vb = (vv * b_w).astype(BF16)
        rhs = jnp.concatenate(
            [jnp.concatenate([kg[:, h * DN_DIM:(h + 1) * DN_DIM],
                              vb[:, h * DN_DIM:(h + 1) * DN_DIM]], axis=1) for h in range(NH)],
            axis=0)
        wu = _dot(block_diag(t.astype(BF16)), rhs)
        g_last = g_w[CHUNK - 1:CHUNK, :]
        kd = (kn * jnp.exp(g_last - g_w)).astype(BF16)
        yield
        wu = wu.astype(BF16)
        aw = _dot(block_diag(attn.astype(BF16)), wu)
        pq = [_dot_tn(kd[:, h * DN_DIM:(h + 1) * DN_DIM], wu[h * CHUNK:(h + 1) * CHUNK, :])
              for h in range(NH)]
        yield
        z_scr[c] = jnp.concatenate(
            [aw[h * CHUNK:(h + 1) * CHUNK, DN_DIM:2 * DN_DIM] for h in range(NH)], axis=1)
        q_eff = qn * eg - jnp.concatenate(
            [aw[h * CHUNK:(h + 1) * CHUNK, 0:DN_DIM] for h in range(NH)], axis=1)
        for pr in range(NH // 2):
            h0, h1 = 2 * pr, 2 * pr + 1
            pw_scr[c, pr, 0:DN_DIM, :] = jnp.concatenate(
                [pq[h0][:, 0:DN_DIM], pq[h1][:, 0:DN_DIM]], axis=1).astype(BF16)
            pw_scr[c, pr, DN_DIM:DN_DIM + CHUNK, :] = q_eff[:, pr * PAIR:(pr + 1) * PAIR].astype(BF16)
            q_scr[c, pr] = jnp.concatenate(
                [pq[h0][:, DN_DIM:2 * DN_DIM], pq[h1][:, DN_DIM:2 * DN_DIM]], axis=1)

    dn_state = [sdn_scr[pr] for pr in range(NH // 2)]

    def dn_apply(chunks):
        zero = jnp.zeros((DN_DIM, DN_DIM), BF16)
        for c in chunks:
            rows = rows_of(c)
            eg_last = jnp.exp(bc_scr[(c + 1) * CHUNK - 1:(c + 1) * CHUNK, BC_G:BC_G + DN_WIDTH])
            prods = []
            for pr in range(NH // 2):
                s_b = dn_state[pr].astype(BF16)
                rhs = jnp.concatenate(
                    [jnp.concatenate([s_b[:, 0:DN_DIM], zero], axis=1),
                     jnp.concatenate([zero, s_b[:, DN_DIM:PAIR]], axis=1)], axis=0)
                prods.append(_dot(pw_scr[c, pr], rhs))
            yield
            for pr in range(NH // 2):
                dn_state[pr] = ((dn_state[pr] * eg_last[:, pr * PAIR:(pr + 1) * PAIR]
                                 - prods[pr][0:DN_DIM]) + q_scr[c, pr])
            o = jnp.concatenate([r[DN_DIM:DN_DIM + CHUNK] for r in prods], axis=1) + z_scr[c]
            z = proj_scr[rows, P_Z:P_Z + DN_WIDTH]
            normed = []
            for h in range(NH):
                oh = o[:, h * DN_DIM:(h + 1) * DN_DIM]
                normed.append(oh * lax.rsqrt(jnp.mean(oh * oh, -1, keepdims=True) + EPS))
            on = jnp.concatenate(normed, axis=1) * dng_ref[...]
            o_ref[0, rows, 0:DN_WIDTH] = (on * _silu(z)).astype(o_ref.dtype)
            yield

    gla_state = [sgl_scr[...]]

    def gla_chunk(c):
        rows = rows_of(c)
        row, col = quad_masks()
        b = gb_scr[rows, :]
        q = proj_scr[rows, P_GQ:P_GQ + GLA_KW] * (GLA_KEY ** -0.5)
        k = proj_scr[rows, P_GK:P_GK + GLA_KW]
        v = proj_scr[rows, P_GV:P_GV + GLA_WIDTH].astype(BF16)
        b_last = b[CHUNK - 1:CHUNK, :]
        q_dec = (q * jnp.exp(b)).astype(BF16)
        k_inv = (k * jnp.exp(-b)).astype(BF16)
        k_dec = (k * jnp.exp(b_last - b)).astype(BF16)
        v_rows = _rows_by_head(v, GLA_VAL)
        scores = _dot_nt(q_dec, block_diag(k_inv))
        kt = _dot_tn(v_rows, block_diag(k_dec))
        yield
        attn = jnp.where(row >= col, scores, 0.0)
        st = gla_state[0]
        o = (_dot_nt(block_diag(q_dec), st.astype(BF16))
             + _dot(block_diag(attn.astype(BF16)), v_rows))
        gla_state[0] = st * jnp.exp(b_last) + kt
        yield
        gg = _rows_by_head(proj_scr[rows, P_GG:P_GG + GLA_WIDTH], GLA_VAL)
        on = o * lax.rsqrt(jnp.mean(o * o, -1, keepdims=True) + EPS) * glg_ref[...]
        res = (on * _silu(gg)).astype(o_ref.dtype)
        for h in range(NH):
            o_ref[0, rows, DN_WIDTH + h * GLA_VAL:DN_WIDTH + (h + 1) * GLA_VAL] = (
                res[h * CHUNK:(h + 1) * CHUNK, :])

    half = n_chunks // 2
    _drive([dn_prepare(c) for c in range(half)] + [gla_chunk(c) for c in range(half)])
    _drive([dn_prepare(c) for c in range(half, n_chunks)] + [dn_apply(range(half))]
           + [gla_chunk(c) for c in range(half, n_chunks)])
    _drive([dn_apply(range(half, n_chunks))])
    for pr in range(NH // 2):
        sdn_scr[pr] = dn_state[pr]
    sgl_scr[...] = gla_state[0]


def _mixer_constants():
    idx = np.arange(TRI_SPAN)
    tri = (idx[:, None] >= idx[None, :]) & (idx[:, None] // CHUNK == idx[None, :] // CHUNK)
    sel = np.zeros((2 * SMALL, BC_W), np.float32)
    for part in range(2):
        for h in range(NH):
            sel[part * SMALL + SM_A + h, BC_GQ + h * CHUNK:BC_GQ + (h + 1) * CHUNK] = 1.0
            sel[part * SMALL + SM_A + h, BC_G + h * DN_DIM:BC_G + (h + 1) * DN_DIM] = 1.0
            sel[part * SMALL + SM_B + h, BC_B + h * DN_DIM:BC_B + (h + 1) * DN_DIM] = 1.0
    r = np.arange(QUAD)
    mask_q = (r[:, None] // CHUNK == r[None, :] // CHUNK)
    l2 = np.arange(2 * SMALL)
    mask_r = (l2[None, :] % SMALL == SM_A + r[:, None] // CHUNK)
    as_bf16 = lambda a: jnp.asarray(a.astype(np.float32), dtype=BF16)
    return as_bf16(tri), as_bf16(sel), as_bf16(mask_q), as_bf16(mask_r)


def _mixer(x, mod3, ln0_g, ln0_b, w_in_r, dn_conv, alog_p, dtb_p, dn_norm_g, wg2_p, b_gate,
           gla_norm_g, tt):
    bsz, t, _ = x.shape
    n_chunks = tt // CHUNK
    assert tt % TRI_SPAN == 0 and n_chunks % 2 == 0
    const = lambda b, i: (0, 0)
    tri, sel, mask_q, mask_r = _mixer_constants()
    return pl.pallas_call(
        functools.partial(_mixer_kernel, tt=tt),
        grid=(bsz, t // tt),
        in_specs=[pl.BlockSpec((1, tt, D_MODEL), lambda b, i: (b, i, 0)),
                  pl.BlockSpec((1, 1, 6 * D_MODEL), lambda b, i: (b, 0, 0)),
                  pl.BlockSpec((1, D_MODEL), const),
                  pl.BlockSpec((1, D_MODEL), const),
                  pl.BlockSpec((D_MODEL, PROJ_W), const, pipeline_mode=pl.Buffered(1)),
                  pl.BlockSpec((TRI_SPAN, TRI_SPAN), const),
                  pl.BlockSpec((2 * SMALL, BC_W), const),
                  pl.BlockSpec((QUAD, QUAD), const),
                  pl.BlockSpec((QUAD, 2 * SMALL), const),
                  pl.BlockSpec((SHORT_CONV, 3 * DN_WIDTH), const),
                  pl.BlockSpec((1, SMALL), const),
                  pl.BlockSpec((1, SMALL), const),
                  pl.BlockSpec((1, DN_WIDTH), const),
                  pl.BlockSpec((SMALL, GLA_KW), const),
                  pl.BlockSpec((1, GLA_KW), const),
                  pl.BlockSpec((1, GLA_VAL), const)],
        out_specs=pl.BlockSpec((1, tt, D_MODEL), lambda b, i: (b, i, 0)),
        out_shape=jax.ShapeDtypeStruct((bsz, t, D_MODEL), BF16),
        scratch_shapes=[pltpu.VMEM((tt, D_MODEL), BF16),
                        pltpu.VMEM((tt, C_SM - C_Z), F32),
                        pltpu.VMEM((tt + CARRY, 3 * DN_WIDTH), F32),
                        pltpu.VMEM((tt, DN_WIDTH), F32),
                        pltpu.VMEM((tt, DN_WIDTH), F32),
                        pltpu.VMEM((tt, DN_WIDTH), F32),
                        pltpu.VMEM((tt, BC_W), F32),
                        pltpu.VMEM((tt, GLA_KW), F32),
                        pltpu.VMEM((n_chunks, CHUNK, QUAD), F32),
                        pltpu.VMEM((n_chunks, NH // 2, DN_DIM + CHUNK, PAIR), BF16),
                        pltpu.VMEM((n_chunks, NH // 2, DN_DIM, PAIR), F32),
                        pltpu.VMEM((n_chunks, CHUNK, DN_WIDTH), F32),
                        pltpu.VMEM((NH // 2, DN_DIM, PAIR), F32),
                        pltpu.VMEM((GLA_VAL, GLA_KW), F32)],
        compiler_params=pltpu.CompilerParams(
            dimension_semantics=("arbitrary", "arbitrary"), vmem_limit_bytes=VMEM_LIMIT),
        name="mixer",
    )(x, mod3, ln0_g.reshape(1, -1), ln0_b.reshape(1, -1), w_in_r, tri, sel, mask_q, mask_r,
      dn_conv, alog_p, dtb_p, jnp.tile(dn_norm_g.reshape(1, -1), (1, NH)), wg2_p,
      b_gate.reshape(1, -1), gla_norm_g.reshape(1, -1))


def _ffn_kernel(x_ref, o_ref, mod_ref, g0_ref, b0_ref, wo_ref, g1_ref, b1_ref,
                wup_ref, cw_ref, cb_ref, wdn_ref, g2_ref, b2_ref,
                out_ref, ue_scr, act_scr, *, tf, ff_chunks):
    t_idx = pl.program_id(1)

    @pl.when(t_idx == 0)
    def _():
        ue_scr[:, 0:CARRY, :] = jnp.zeros((ue_scr.shape[0], CARRY, ue_scr.shape[2]), F32)

    gt_a = mod_ref[0, :, 2 * D_MODEL:3 * D_MODEL]
    sh_f = mod_ref[0, :, 3 * D_MODEL:4 * D_MODEL]
    sc_f = mod_ref[0, :, 4 * D_MODEL:5 * D_MODEL]
    gt_f = mod_ref[0, :, 5 * D_MODEL:6 * D_MODEL]

    x0 = _layer_norm(x_ref[0], g0_ref[...], b0_ref[...])
    y = _dot(o_ref[0], wo_ref[...])
    x1 = _layer_norm(ALPHA * x0 + (1.0 + gt_a) * y, g1_ref[...], b1_ref[...])
    h2 = (x1 * (1.0 + sc_f) + sh_f).astype(BF16)

    y2 = None
    for j, (f0, fw) in enumerate(ff_chunks):
        for part in range(2):
            c0 = part * D_FF + f0
            ue_scr[2 * j + part, CARRY:CARRY + tf, 0:fw] = _dot(h2, wup_ref[:, c0:c0 + fw])
        for s0 in range(0, fw, FF_SLAB):
            halves = []
            for part in range(2):
                c0 = part * D_FF + f0 + s0
                slot = 2 * j + part
                acc = cb_ref[:, c0:c0 + FF_SLAB]
                taps = _shifted_taps(ue_scr[slot, :, s0:s0 + FF_SLAB], FFN_CONV, tf)
                for k, tap in enumerate(taps):
                    acc = acc + tap * cw_ref[k:k + 1, c0:c0 + FF_SLAB]
                halves.append(acc)
            act_scr[:, s0:s0 + FF_SLAB] = (_silu(halves[0]) * halves[1]).astype(BF16)
        for part in range(2):
            slot = 2 * j + part
            ue_scr[slot, 0:CARRY, 0:fw] = ue_scr[slot, tf:tf + CARRY, 0:fw]
        part_y = _dot(act_scr[:, 0:fw], wdn_ref[f0:f0 + fw, :])
        y2 = part_y if y2 is None else y2 + part_y
    out_ref[0] = _layer_norm(ALPHA * x1 + (1.0 + gt_f) * y2, g2_ref[...], b2_ref[...])


def _ffn(x, o, mod3, ln0_g, ln0_b, w_o, ln1_g, ln1_b, w_up, conv_w, conv_b, w_down,
         ln2_g, ln2_b, tf, ff_chunks):
    bsz, t, _ = x.shape
    const = lambda b, i: (0, 0)
    once = pl.Buffered(1)
    vec = lambda a: a.reshape(1, -1)
    fw_max = max(fw for _, fw in ff_chunks)
    assert sum(fw for _, fw in ff_chunks) == D_FF and all(fw % FF_SLAB == 0 for _, fw in ff_chunks)
    return pl.pallas_call(
        functools.partial(_ffn_kernel, tf=tf, ff_chunks=ff_chunks),
        grid=(bsz, t // tf),
        in_specs=[pl.BlockSpec((1, tf, D_MODEL), lambda b, i: (b, i, 0)),
                  pl.BlockSpec((1, tf, D_MODEL), lambda b, i: (b, i, 0)),
                  pl.BlockSpec((1, 1, 6 * D_MODEL), lambda b, i: (b, 0, 0)),
                  pl.BlockSpec((1, D_MODEL), const),
                  pl.BlockSpec((1, D_MODEL), const),
                  pl.BlockSpec((D_MODEL, D_MODEL), const, pipeline_mode=once),
                  pl.BlockSpec((1, D_MODEL), const),
                  pl.BlockSpec((1, D_MODEL), const),
                  pl.BlockSpec((D_MODEL, 2 * D_FF), const, pipeline_mode=once),
                  pl.BlockSpec((FFN_CONV, 2 * D_FF), const),
                  pl.BlockSpec((1, 2 * D_FF), const),
                  pl.BlockSpec((D_FF, D_MODEL), const, pipeline_mode=once),
                  pl.BlockSpec((1, D_MODEL), const),
                  pl.BlockSpec((1, D_MODEL), const)],
        out_specs=pl.BlockSpec((1, tf, D_MODEL), lambda b, i: (b, i, 0)),
        out_shape=jax.ShapeDtypeStruct((bsz, t, D_MODEL), F32),
        scratch_shapes=[pltpu.VMEM((2 * len(ff_chunks), tf + CARRY, fw_max), F32),
                        pltpu.VMEM((tf, fw_max), BF16)],
        compiler_params=pltpu.CompilerParams(
            dimension_semantics=("arbitrary", "arbitrary"), vmem_limit_bytes=VMEM_LIMIT),
        name="ffn",
    )(x, o, mod3, vec(ln0_g), vec(ln0_b), w_o, vec(ln1_g), vec(ln1_b), w_up, conv_w,
      vec(conv_b), w_down, vec(ln2_g), vec(ln2_b))


def _pad_lanes(a, offset, width):
    a2 = a.reshape(1, -1)
    return jnp.pad(a2, ((0, 0), (offset, width - offset - a2.shape[1])))


def kernel(x, c, ln0_g, ln0_b, w_ada, b_ada, w_in, dn_conv, dn_a_log, dn_dt_bias, dn_norm_g,
           gla_w_gate2, gla_b_gate, gla_norm_g, w_o, ln1_g, ln1_b, ffn_w_up, ffn_conv, ffn_conv_b,
           ffn_w_down, ln2_g, ln2_b):
    bsz, t, _ = x.shape
    tt, tf = 512, 512
    ff_chunks = ((0, 1536), (1536, 1280))

    wi = w_in[0]
    o_q = 0
    o_a = 4 * DN_WIDTH
    o_b = o_a + DN_HEADS
    o_gq = o_b + DN_HEADS
    o_gr = o_gq + 2 * GLA_KW + 2 * GLA_WIDTH
    small_cols = jnp.concatenate(
        [wi[:, o_a:o_a + DN_HEADS], wi[:, o_b:o_b + DN_HEADS], wi[:, o_gr:o_gr + GLA_RANK],
         jnp.zeros((D_MODEL, SMALL - 2 * DN_HEADS - GLA_RANK), wi.dtype)], axis=1)
    w_in_r = jnp.concatenate([wi[:, o_q:o_a], wi[:, o_gq:o_gr], small_cols], axis=1).astype(BF16)
    alog_p = _pad_lanes(dn_a_log[0], SM_A, SMALL)
    dtb_p = _pad_lanes(dn_dt_bias[0], SM_A, SMALL)
    wg2_p = jnp.pad(gla_w_gate2[0], ((SM_R, SMALL - SM_R - GLA_RANK), (0, 0))).astype(BF16)

    mod = _ada(c, w_ada[0], b_ada[0])
    mod3 = mod.reshape(bsz, 1, 6 * D_MODEL)
    o = _mixer(x, mod3, ln0_g, ln0_b, w_in_r, dn_conv[0], alog_p, dtb_p, dn_norm_g[0], wg2_p,
               gla_b_gate[0], gla_norm_g[0], tt)
    return _ffn(x, o, mod3, ln0_g, ln0_b, w_o[0].astype(BF16), ln1_g[0], ln1_b[0],
                ffn_w_up[0].astype(BF16), ffn_conv[0], ffn_conv_b[0],
                ffn_w_down[0].astype(BF16), ln2_g[0], ln2_b[0], tf, ff_chunks)
```

```python
import functools

import jax
import jax.numpy as jnp
import numpy as np
from jax import lax
from jax.experimental import pallas as pl
from jax.experimental.pallas import tpu as pltpu

F32 = jnp.float32
BF16 = jnp.bfloat16

D_MODEL = 1024
DN_HEADS = 4
DN_DIM = 128
DN_WIDTH = DN_HEADS * DN_DIM
SHORT_CONV = 4
GLA_HEADS = 4
GLA_KEY = 64
GLA_VAL = 128
GLA_KW = GLA_HEADS * GLA_KEY
GLA_WIDTH = GLA_HEADS * GLA_VAL
GLA_RANK = 16
GLA_TAU = 16.0
CHUNK = 64
D_FF = 2816
FFN_CONV = 3
ALPHA = 2.0 ** 0.25
EPS = 1e-6
NH = 4
QUAD = NH * CHUNK
PAIR = 2 * DN_DIM
TRI_SPAN = 256
FF_SLAB = 256
C_QKV = 0
C_Z = 3 * DN_WIDTH
C_GQ = C_Z + DN_WIDTH
C_GK = C_GQ + GLA_KW
C_GV = C_GK + GLA_KW
C_GG = C_GV + GLA_WIDTH
C_SM = C_GG + GLA_WIDTH
SMALL = 128
PROJ_W = C_SM + SMALL
SM_A, SM_B, SM_R = 0, DN_HEADS, 2 * DN_HEADS
P_Z, P_GQ, P_GK, P_GV, P_GG = 0, C_GQ - C_Z, C_GK - C_Z, C_GV - C_Z, C_GG - C_Z

BC_GQ = 0
BC_G = QUAD
BC_B = BC_G + DN_WIDTH
BC_W = BC_B + DN_WIDTH

CARRY = 8
VMEM_LIMIT = 56 * 1024 * 1024


def _dot(a, b):
    return jnp.dot(a, b, preferred_element_type=F32)


def _dot_nt(a, b):
    return lax.dot_general(a, b, (((1,), (1,)), ((), ())), preferred_element_type=F32)


def _dot_tn(a, b):
    return lax.dot_general(a, b, (((0,), (0,)), ((), ())), preferred_element_type=F32)


def _bdot(a, b):
    return _dot(a.astype(BF16), b.astype(BF16))


def _split(a):
    hi = a.astype(BF16)
    lo = (a - hi.astype(F32)).astype(BF16)
    return hi, lo


def _dot_exact_lhs(a_bf16, b):
    bh, bl = _split(b)
    return _dot(a_bf16, bh) + _dot(a_bf16, bl)


def _silu_of_twice(h):
    return h + h * jnp.tanh(h)


def _silu(x):
    return _silu_of_twice(0.5 * x)


def _shifted_taps(x_ext, n_taps, rows):
    taps = []
    for k in range(n_taps):
        shift = n_taps - 1 - k
        rolled = x_ext if shift == 0 else pltpu.roll(x_ext, shift, 0)
        taps.append(rolled[CARRY:CARRY + rows])
    return taps


def _layer_norm(x, g, b):
    mu = jnp.mean(x, axis=-1, keepdims=True)
    xc = x - mu
    var = jnp.mean(xc * xc, axis=-1, keepdims=True)
    return xc * lax.rsqrt(var + EPS) * g + b


def _rows_by_head(a, width):
    return jnp.concatenate([a[:, h * width:(h + 1) * width] for h in range(NH)], axis=0)


def _stack4(a):
    return jnp.concatenate([a, a, a, a], axis=0)


def _ada_kernel(c_ref, w_ref, b_ref, o_ref):
    cond = _silu(c_ref[...])
    o_ref[...] = _bdot(cond, w_ref[...]) + b_ref[...]


def _ada(c, w_ada, b_ada):
    bsz = c.shape[0]
    n_out = w_ada.shape[1]
    blk = D_MODEL
    return pl.pallas_call(
        _ada_kernel,
        grid=(n_out // blk,),
        in_specs=[pl.BlockSpec((bsz, D_MODEL), lambda j: (0, 0)),
                  pl.BlockSpec((D_MODEL, blk), lambda j: (0, j)),
                  pl.BlockSpec((1, blk), lambda j: (0, j))],
        out_specs=pl.BlockSpec((bsz, blk), lambda j: (0, j)),
        out_shape=jax.ShapeDtypeStruct((bsz, n_out), F32),
        name="ada_mod",
    )(c, w_ada, b_ada.reshape(1, n_out))


_DONE = object()


def _drive(tasks):
    tasks = list(tasks)
    while tasks:
        tasks = [task for task in tasks if next(task, _DONE) is not _DONE]


def _mixer_kernel(x_ref, mod_ref, g0_ref, b0_ref, win_ref, tri_ref, sel_ref, mq_ref, mr_ref,
                  cw_ref, alog_ref, dtb_ref, dng_ref, wg2_ref, bg_ref, glg_ref,
                  o_ref,
                  h_scr, proj_scr, xe_scr, qn_scr, kn_scr, v_scr, bc_scr, gb_scr, dec_scr,
                  pw_scr, q_scr, z_scr, sdn_scr, sgl_scr, *, tt):
    t_idx = pl.program_id(1)
    n_chunks = tt // CHUNK

    @pl.when(t_idx == 0)
    def _():
        xe_scr[0:CARRY, :] = jnp.zeros((CARRY, 3 * DN_WIDTH), F32)
        sdn_scr[...] = jnp.zeros_like(sdn_scr)
        sgl_scr[...] = jnp.zeros_like(sgl_scr)

    x0 = _layer_norm(x_ref[0], g0_ref[...], b0_ref[...])
    h_scr[...] = (x0 * (1.0 + mod_ref[0, :, D_MODEL:2 * D_MODEL])
                  + mod_ref[0, :, 0:D_MODEL]).astype(BF16)

    def project(c0, width):
        return _dot(h_scr[...], win_ref[:, c0:c0 + width])

    small = project(C_SM, SMALL)

    for j in range(3 * DN_HEADS):
        if j % DN_HEADS == 0:
            xe_scr[CARRY:CARRY + tt, j * DN_DIM:j * DN_DIM + DN_WIDTH] = project(
                C_QKV + j * DN_DIM, DN_WIDTH)
        cs = slice(j * DN_DIM, (j + 1) * DN_DIM)
        conv = None
        for k, tap in enumerate(_shifted_taps(xe_scr[:, cs], SHORT_CONV, tt)):
            term = tap * cw_ref[k:k + 1, cs]
            conv = term if conv is None else conv + term
        act = _silu_of_twice(conv)
        hs = slice((j % DN_HEADS) * DN_DIM, (j % DN_HEADS + 1) * DN_DIM)
        if j < DN_HEADS:
            qn_scr[:, hs] = act * (lax.rsqrt(jnp.sum(act * act, -1, keepdims=True) + EPS)
                                   * (DN_DIM ** -0.5))
        elif j < 2 * DN_HEADS:
            kn_scr[:, hs] = act * lax.rsqrt(jnp.sum(act * act, -1, keepdims=True) + EPS)
        else:
            v_scr[:, hs] = act
    xe_scr[0:CARRY, :] = xe_scr[tt:tt + CARRY, :]
    proj_scr[...] = project(C_Z, C_SM - C_Z)

    lane = lax.broadcasted_iota(jnp.int32, (tt, SMALL), 1)
    log_a = -jnp.exp(alog_ref[...]) * jax.nn.softplus(small + dtb_ref[...])
    log_a = jnp.where(lane < DN_HEADS, log_a, 0.0)
    def chunk_cumsum(a):
        span = tri_ref.shape[0]
        return jnp.concatenate(
            [_dot_exact_lhs(tri_ref[...], a[r:r + span]) for r in range(0, tt, span)], axis=0)

    def quad_masks():
        row = lax.broadcasted_iota(jnp.int32, (CHUNK, QUAD), 0)
        col = lax.broadcasted_iota(jnp.int32, (CHUNK, QUAD), 1) & (CHUNK - 1)
        return row, col

    g_cum = chunk_cumsum(log_a)
    sc = jnp.where(lane < DN_HEADS, g_cum, jax.nn.sigmoid(small))
    sc_hi, sc_lo = _split(sc)
    sl = jnp.concatenate([sc_hi, sc_lo], axis=1)
    bc_scr[...] = _dot(sl, sel_ref[...])

    row_q, col_q = quad_masks()
    ones_lhs = jnp.ones((CHUNK, 2 * SMALL), BF16)
    for c in range(n_chunks):
        rows = slice(c * CHUNK, (c + 1) * CHUNK)
        g_row = _dot_nt(ones_lhs, _stack4(sl[rows]) * mr_ref[...])
        g_col = bc_scr[rows, BC_GQ:BC_GQ + QUAD]
        dec_scr[c] = jnp.exp(jnp.where(row_q >= col_q, g_col - g_row, -jnp.inf))

    gate_pre = _bdot(small, wg2_ref[...]) + bg_ref[...]
    log_alpha = jax.nn.log_sigmoid(gate_pre) * (1.0 / GLA_TAU)
    gb_scr[...] = chunk_cumsum(log_alpha)

    def rows_of(c):
        return slice(c * CHUNK, (c + 1) * CHUNK)

    def block_diag(a_bf16):
        zero = jnp.zeros((CHUNK, 2 * CHUNK), BF16)
        keep_lo = mq_ref[0:CHUNK, 0:2 * CHUNK]
        keep_hi = mq_ref[CHUNK:2 * CHUNK, 0:2 * CHUNK]
        left, right = a_bf16[:, 0:2 * CHUNK], a_bf16[:, 2 * CHUNK:QUAD]
        return jnp.concatenate(
            [jnp.concatenate([left * keep_lo, zero], axis=1),
             jnp.concatenate([left * keep_hi, zero], axis=1),
             jnp.concatenate([zero, right * keep_lo], axis=1),
             jnp.concatenate([zero, right * keep_hi], axis=1)], axis=0)

    def dn_prepare(c):
        rows = rows_of(c)
        row, col = quad_masks()
        kn = kn_scr[rows, :]
        qn = qn_scr[rows, :]
        vv = v_scr[rows, :]
        g_w = bc_scr[rows, BC_G:BC_G + DN_WIDTH]
        b_w = bc_scr[rows, BC_B:BC_B + DN_WIDTH]
        decay = dec_scr[c]
        kb = kn * b_w
        kn_b = kn.astype(BF16)
        zero = jnp.zeros((CHUNK, DN_DIM), BF16)
        k_bd = jnp.concatenate(
            [jnp.concatenate([kn_b[:, h * DN_DIM:(h + 1) * DN_DIM] if g == h else zero
                              for g in range(NH)], axis=1) for h in range(NH)], axis=0)
        aqk = _dot_nt(jnp.concatenate([kb, qn], axis=0).astype(BF16), k_bd)
        yield
        m_low = jnp.where(row > col, aqk[0:CHUNK] * decay, 0.0)
        attn = aqk[CHUNK:2 * CHUNK] * decay
        p = -m_low
        t = jnp.where(row == col, 1.0, 0.0) + p
        n_levels = CHUNK.bit_length() - 1
        for lvl in range(n_levels):
            bd = block_diag(p.astype(BF16))
            if lvl == 0:
                p = _dot(p.astype(BF16), bd)
                yield
            elif lvl < n_levels - 1:
                both = _dot(jnp.concatenate([p, t], axis=0).astype(BF16), bd)
                yield
                p = both[0:CHUNK]
                t = t + both[CHUNK:2 * CHUNK]
            else:
                t_inc = _dot(t.astype(BF16), bd)
                yield
                t = t + t_inc
        eg = jnp.exp(g_w)
        kg = (kb * eg).astype(BF16)
        vb = (vv * b_w).astype(BF16)
        rhs = jnp.concatenate(
            [jnp.concatenate([kg[:, h * DN_DIM:(h + 1) * DN_DIM],
                              vb[:, h * DN_DIM:(h + 1) * DN_DIM]], axis=1) for h in range(NH)],
            axis=0)
        wu = _dot(block_diag(t.astype(BF16)), rhs)
        g_last = g_w[CHUNK - 1:CHUNK, :]
        kd = (kn * jnp.exp(g_last - g_w)).astype(BF16)
        yield
        wu = wu.astype(BF16)
        aw = _dot(block_diag(attn.astype(BF16)), wu)
        pq = [_dot_tn(kd[:, h * DN_DIM:(h + 1) * DN_DIM], wu[h * CHUNK:(h + 1) * CHUNK, :])
              for h in range(NH)]
        yield
        z_scr[c] = jnp.concatenate(
            [aw[h * CHUNK:(h + 1) * CHUNK, DN_DIM:2 * DN_DIM] for h in range(NH)], axis=1)
        q_eff = qn * eg - jnp.concatenate(
            [aw[h * CHUNK:(h + 1) * CHUNK, 0:DN_DIM] for h in range(NH)], axis=1)
        for pr in range(NH // 2):
            h0, h1 = 2 * pr, 2 * pr + 1
            pw_scr[c, pr, 0:DN_DIM, :] = jnp.concatenate(
                [pq[h0][:, 0:DN_DIM], pq[h1][:, 0:DN_DIM]], axis=1).astype(BF16)
            pw_scr[c, pr, DN_DIM:DN_DIM + CHUNK, :] = q_eff[:, pr * PAIR:(pr + 1) * PAIR].astype(BF16)
            q_scr[c, pr] = jnp.concatenate(
                [pq[h0][:, DN_DIM:2 * DN_DIM], pq[h1][:, DN_DIM:2 * DN_DIM]], axis=1)

    dn_state = [sdn_scr[pr] for pr in range(NH // 2)]

    def dn_apply(chunks):
        zero = jnp.zeros((DN_DIM, DN_DIM), BF16)
        for c in chunks:
            rows = rows_of(c)
            eg_last = jnp.exp(bc_scr[(c + 1) * CHUNK - 1:(c + 1) * CHUNK, BC_G:BC_G + DN_WIDTH])
            prods = []
            for pr in range(NH // 2):
                s_b = dn_state[pr].astype(BF16)
                rhs = jnp.concatenate(
                    [jnp.concatenate([s_b[:, 0:DN_DIM], zero], axis=1),
                     jnp.concatenate([zero, s_b[:, DN_DIM:PAIR]], axis=1)], axis=0)
                prods.append(_dot(pw_scr[c, pr], rhs))
            yield
            for pr in range(NH // 2):
                dn_state[pr] = ((dn_state[pr] * eg_last[:, pr * PAIR:(pr + 1) * PAIR]
                                 - prods[pr][0:DN_DIM]) + q_scr[c, pr])
            o = jnp.concatenate([r[DN_DIM:DN_DIM + CHUNK] for r in prods], axis=1) + z_scr[c]
            z = proj_scr[rows, P_Z:P_Z + DN_WIDTH]
            normed = []
            for h in range(NH):
                oh = o[:, h * DN_DIM:(h + 1) * DN_DIM]
                normed.append(oh * lax.rsqrt(jnp.mean(oh * oh, -1, keepdims=True) + EPS))
            on = jnp.concatenate(normed, axis=1) * dng_ref[...]
            o_ref[0, rows, 0:DN_WIDTH] = (on * _silu_of_twice(z)).astype(o_ref.dtype)
            yield

    gla_state = [sgl_scr[...]]

    def gla_chunk(c):
        rows = rows_of(c)
        row, col = quad_masks()
        b = gb_scr[rows, :]
        q = proj_scr[rows, P_GQ:P_GQ + GLA_KW] * (GLA_KEY ** -0.5)
        k = proj_scr[rows, P_GK:P_GK + GLA_KW]
        v = proj_scr[rows, P_GV:P_GV + GLA_WIDTH].astype(BF16)
        b_last = b[CHUNK - 1:CHUNK, :]
        q_dec = (q * jnp.exp(b)).astype(BF16)
        k_inv = (k * jnp.exp(-b)).astype(BF16)
        k_dec = (k * jnp.exp(b_last - b)).astype(BF16)
        v_rows = _rows_by_head(v, GLA_VAL)
        scores = _dot_nt(q_dec, block_diag(k_inv))
        kt = _dot_tn(v_rows, block_diag(k_dec))
        yield
        attn = jnp.where(row >= col, scores, 0.0)
        st = gla_state[0]
        o = (_dot_nt(block_diag(q_dec), st.astype(BF16))
             + _dot(block_diag(attn.astype(BF16)), v_rows))
        gla_state[0] = st * jnp.exp(b_last) + kt
        yield
        gg = _rows_by_head(proj_scr[rows, P_GG:P_GG + GLA_WIDTH], GLA_VAL)
        on = o * lax.rsqrt(jnp.mean(o * o, -1, keepdims=True) + EPS) * glg_ref[...]
        res = (on * _silu_of_twice(gg)).astype(o_ref.dtype)
        for h in range(NH):
            o_ref[0, rows, DN_WIDTH + h * GLA_VAL:DN_WIDTH + (h + 1) * GLA_VAL] = (
                res[h * CHUNK:(h + 1) * CHUNK, :])

    _drive([dn_prepare(c) for c in range(n_chunks)])
    _drive([dn_apply(range(n_chunks))] + [gla_chunk(c) for c in range(n_chunks)])
    for pr in range(NH // 2):
        sdn_scr[pr] = dn_state[pr]
    sgl_scr[...] = gla_state[0]


def _mixer_constants():
    idx = np.arange(TRI_SPAN)
    tri = (idx[:, None] >= idx[None, :]) & (idx[:, None] // CHUNK == idx[None, :] // CHUNK)
    sel = np.zeros((2 * SMALL, BC_W), np.float32)
    for part in range(2):
        for h in range(NH):
            sel[part * SMALL + SM_A + h, BC_GQ + h * CHUNK:BC_GQ + (h + 1) * CHUNK] = 1.0
            sel[part * SMALL + SM_A + h, BC_G + h * DN_DIM:BC_G + (h + 1) * DN_DIM] = 1.0
            sel[part * SMALL + SM_B + h, BC_B + h * DN_DIM:BC_B + (h + 1) * DN_DIM] = 1.0
    r = np.arange(QUAD)
    mask_q = (r[:, None] // CHUNK == r[None, :] // CHUNK)
    l2 = np.arange(2 * SMALL)
    mask_r = (l2[None, :] % SMALL == SM_A + r[:, None] // CHUNK)
    as_bf16 = lambda a: jnp.asarray(a.astype(np.float32), dtype=BF16)
    return as_bf16(tri), as_bf16(sel), as_bf16(mask_q), as_bf16(mask_r)


def _mixer(x, mod3, ln0_g, ln0_b, w_in_r, dn_conv, alog_p, dtb_p, dn_norm_g, wg2_p, b_gate,
           gla_norm_g, tt):
    bsz, t, _ = x.shape
    n_chunks = tt // CHUNK
    assert tt % TRI_SPAN == 0 and n_chunks % 2 == 0
    const = lambda b, i: (0, 0)
    tri, sel, mask_q, mask_r = _mixer_constants()
    return pl.pallas_call(
        functools.partial(_mixer_kernel, tt=tt),
        grid=(bsz, t // tt),
        in_specs=[pl.BlockSpec((1, tt, D_MODEL), lambda b, i: (b, i, 0)),
                  pl.BlockSpec((1, 1, 6 * D_MODEL), lambda b, i: (b, 0, 0)),
                  pl.BlockSpec((1, D_MODEL), const),
                  pl.BlockSpec((1, D_MODEL), const),
                  pl.BlockSpec((D_MODEL, PROJ_W), const, pipeline_mode=pl.Buffered(1)),
                  pl.BlockSpec((TRI_SPAN, TRI_SPAN), const),
                  pl.BlockSpec((2 * SMALL, BC_W), const),
                  pl.BlockSpec((QUAD, QUAD), const),
                  pl.BlockSpec((QUAD, 2 * SMALL), const),
                  pl.BlockSpec((SHORT_CONV, 3 * DN_WIDTH), const),
                  pl.BlockSpec((1, SMALL), const),
                  pl.BlockSpec((1, SMALL), const),
                  pl.BlockSpec((1, DN_WIDTH), const),
                  pl.BlockSpec((SMALL, GLA_KW), const),
                  pl.BlockSpec((1, GLA_KW), const),
                  pl.BlockSpec((1, GLA_VAL), const)],
        out_specs=pl.BlockSpec((1, tt, D_MODEL), lambda b, i: (b, i, 0)),
        out_shape=jax.ShapeDtypeStruct((bsz, t, D_MODEL), BF16),
        scratch_shapes=[pltpu.VMEM((tt, D_MODEL), BF16),
                        pltpu.VMEM((tt, C_SM - C_Z), F32),
                        pltpu.VMEM((tt + CARRY, 3 * DN_WIDTH), F32),
                        pltpu.VMEM((tt, DN_WIDTH), F32),
                        pltpu.VMEM((tt, DN_WIDTH), F32),
                        pltpu.VMEM((tt, DN_WIDTH), F32),
                        pltpu.VMEM((tt, BC_W), F32),
                        pltpu.VMEM((tt, GLA_KW), F32),
                        pltpu.VMEM((n_chunks, CHUNK, QUAD), F32),
                        pltpu.VMEM((n_chunks, NH // 2, DN_DIM + CHUNK, PAIR), BF16),
                        pltpu.VMEM((n_chunks, NH // 2, DN_DIM, PAIR), F32),
                        pltpu.VMEM((n_chunks, CHUNK, DN_WIDTH), F32),
                        pltpu.VMEM((NH // 2, DN_DIM, PAIR), F32),
                        pltpu.VMEM((GLA_VAL, GLA_KW), F32)],
        compiler_params=pltpu.CompilerParams(
            dimension_semantics=("arbitrary", "arbitrary"), vmem_limit_bytes=VMEM_LIMIT),
        name="mixer",
    )(x, mod3, ln0_g.reshape(1, -1), ln0_b.reshape(1, -1), w_in_r, tri, sel, mask_q, mask_r,
      dn_conv, alog_p, dtb_p, jnp.tile(dn_norm_g.reshape(1, -1), (1, NH)), wg2_p,
      b_gate.reshape(1, -1), gla_norm_g.reshape(1, -1))


def _ffn_kernel(x_ref, o_ref, mod_ref, g0_ref, b0_ref, wo_ref, g1_ref, b1_ref,
                wup_ref, cw_ref, cb_ref, wdn_ref, g2_ref, b2_ref,
                out_ref, ue_scr, act_scr, *, tf, ff_chunks):
    t_idx = pl.program_id(1)

    @pl.when(t_idx == 0)
    def _():
        ue_scr[:, 0:CARRY, :] = jnp.zeros((ue_scr.shape[0], CARRY, ue_scr.shape[2]), F32)

    gt_a = mod_ref[0, :, 2 * D_MODEL:3 * D_MODEL]
    sh_f = mod_ref[0, :, 3 * D_MODEL:4 * D_MODEL]
    sc_f = mod_ref[0, :, 4 * D_MODEL:5 * D_MODEL]
    gt_f = mod_ref[0, :, 5 * D_MODEL:6 * D_MODEL]

    x0 = _layer_norm(x_ref[0], g0_ref[...], b0_ref[...])
    y = _dot(o_ref[0], wo_ref[...])
    x1 = _layer_norm(ALPHA * x0 + (1.0 + gt_a) * y, g1_ref[...], b1_ref[...])
    h2 = (x1 * (1.0 + sc_f) + sh_f).astype(BF16)

    y2 = None
    for j, (f0, fw) in enumerate(ff_chunks):
        for part in range(2):
            c0 = part * D_FF + f0
            ue_scr[2 * j + part, CARRY:CARRY + tf, 0:fw] = _dot(h2, wup_ref[:, c0:c0 + fw])
        for s0 in range(0, fw, FF_SLAB):
            halves = []
            for part in range(2):
                c0 = part * D_FF + f0 + s0
                slot = 2 * j + part
                acc = cb_ref[:, c0:c0 + FF_SLAB]
                taps = _shifted_taps(ue_scr[slot, :, s0:s0 + FF_SLAB], FFN_CONV, tf)
                for k, tap in enumerate(taps):
                    acc = acc + tap * cw_ref[k:k + 1, c0:c0 + FF_SLAB]
                halves.append(acc)
            act_scr[j % 2, :, s0:s0 + FF_SLAB] = (_silu_of_twice(halves[0]) * halves[1]).astype(BF16)
        for part in range(2):
            slot = 2 * j + part
            ue_scr[slot, 0:CARRY, 0:fw] = ue_scr[slot, tf:tf + CARRY, 0:fw]
        part_y = _dot(act_scr[j % 2, :, 0:fw], wdn_ref[f0:f0 + fw, :])
        y2 = part_y if y2 is None else y2 + part_y
    out_ref[0] = _layer_norm(ALPHA * x1 + (1.0 + gt_f) * y2, g2_ref[...], b2_ref[...])


def _ffn(x, o, mod3, ln0_g, ln0_b, w_o, ln1_g, ln1_b, w_up, conv_w, conv_b, w_down,
         ln2_g, ln2_b, tf, ff_chunks):
    bsz, t, _ = x.shape
    const = lambda b, i: (0, 0)
    once = pl.Buffered(1)
    vec = lambda a: a.reshape(1, -1)
    fw_max = max(fw for _, fw in ff_chunks)
    assert sum(fw for _, fw in ff_chunks) == D_FF and all(fw % FF_SLAB == 0 for _, fw in ff_chunks)
    return pl.pallas_call(
        functools.partial(_ffn_kernel, tf=tf, ff_chunks=ff_chunks),
        grid=(bsz, t // tf),
        in_specs=[pl.BlockSpec((1, tf, D_MODEL), lambda b, i: (b, i, 0)),
                  pl.BlockSpec((1, tf, D_MODEL), lambda b, i: (b, i, 0)),
                  pl.BlockSpec((1, 1, 6 * D_MODEL), lambda b, i: (b, 0, 0)),
                  pl.BlockSpec((1, D_MODEL), const),
                  pl.BlockSpec((1, D_MODEL), const),
                  pl.BlockSpec((D_MODEL, D_MODEL), const, pipeline_mode=once),
                  pl.BlockSpec((1, D_MODEL), const),
                  pl.BlockSpec((1, D_MODEL), const),
                  pl.BlockSpec((D_MODEL, 2 * D_FF), const, pipeline_mode=once),
                  pl.BlockSpec((FFN_CONV, 2 * D_FF), const),
                  pl.BlockSpec((1, 2 * D_FF), const),
                  pl.BlockSpec((D_FF, D_MODEL), const, pipeline_mode=once),
                  pl.BlockSpec((1, D_MODEL), const),
                  pl.BlockSpec((1, D_MODEL), const)],
        out_specs=pl.BlockSpec((1, tf, D_MODEL), lambda b, i: (b, i, 0)),
        out_shape=jax.ShapeDtypeStruct((bsz, t, D_MODEL), F32),
        scratch_shapes=[pltpu.VMEM((2 * len(ff_chunks), tf + CARRY, fw_max), F32),
                        pltpu.VMEM((2, tf, fw_max), BF16)],
        compiler_params=pltpu.CompilerParams(
            dimension_semantics=("arbitrary", "arbitrary"), vmem_limit_bytes=VMEM_LIMIT),
        name="ffn",
    )(x, o, mod3, vec(ln0_g), vec(ln0_b), w_o, vec(ln1_g), vec(ln1_b), w_up, conv_w,
      vec(conv_b), w_down, vec(ln2_g), vec(ln2_b))


def _pad_lanes(a, offset, width):
    a2 = a.reshape(1, -1)
    return jnp.pad(a2, ((0, 0), (offset, width - offset - a2.shape[1])))


def kernel(x, c, ln0_g, ln0_b, w_ada, b_ada, w_in, dn_conv, dn_a_log, dn_dt_bias, dn_norm_g,
           gla_w_gate2, gla_b_gate, gla_norm_g, w_o, ln1_g, ln1_b, ffn_w_up, ffn_conv, ffn_conv_b,
           ffn_w_down, ln2_g, ln2_b):
    bsz, t, _ = x.shape
    tt, tf = 512, 512
    ff_chunks = ((0, 1536), (1536, 1280))

    def layer0(a):
        assert a.shape[0] == 1
        return a.reshape(a.shape[1:])

    (w_ada, b_ada, w_in, dn_conv, dn_a_log, dn_dt_bias, dn_norm_g, gla_w_gate2, gla_b_gate,
     gla_norm_g, w_o, ln1_g, ln1_b, ffn_w_up, ffn_conv, ffn_conv_b, ffn_w_down, ln2_g, ln2_b) = map(
        layer0, (w_ada, b_ada, w_in, dn_conv, dn_a_log, dn_dt_bias, dn_norm_g, gla_w_gate2,
                 gla_b_gate, gla_norm_g, w_o, ln1_g, ln1_b, ffn_w_up, ffn_conv, ffn_conv_b,
                 ffn_w_down, ln2_g, ln2_b))
    wi = w_in
    o_q = 0
    o_a = 4 * DN_WIDTH
    o_b = o_a + DN_HEADS
    o_gq = o_b + DN_HEADS
    o_gr = o_gq + 2 * GLA_KW + 2 * GLA_WIDTH
    small_cols = jnp.concatenate(
        [wi[:, o_a:o_a + DN_HEADS], wi[:, o_b:o_b + DN_HEADS], wi[:, o_gr:o_gr + GLA_RANK],
         jnp.zeros((D_MODEL, SMALL - 2 * DN_HEADS - GLA_RANK), wi.dtype)], axis=1)
    o_z = 3 * DN_WIDTH
    o_gg = o_gr - GLA_WIDTH
    w_in_r = jnp.concatenate(
        [wi[:, o_q:o_z], 0.5 * wi[:, o_z:o_a], wi[:, o_gq:o_gg], 0.5 * wi[:, o_gg:o_gr], small_cols],
        axis=1).astype(BF16)
    dn_conv = 0.5 * dn_conv
    gate_half = jnp.concatenate([jnp.full((D_FF,), 0.5, F32), jnp.ones((D_FF,), F32)])
    ffn_conv = ffn_conv * gate_half
    ffn_conv_b = ffn_conv_b * gate_half
    alog_p = _pad_lanes(dn_a_log, SM_A, SMALL)
    dtb_p = _pad_lanes(dn_dt_bias, SM_A, SMALL)
    wg2_p = jnp.pad(gla_w_gate2, ((SM_R, SMALL - SM_R - GLA_RANK), (0, 0))).astype(BF16)

    mod = _ada(c, w_ada, b_ada)
    mod3 = mod.reshape(bsz, 1, 6 * D_MODEL)
    o = _mixer(x, mod3, ln0_g, ln0_b, w_in_r, dn_conv, alog_p, dtb_p, dn_norm_g, wg2_p,
               gla_b_gate, gla_norm_g, tt)
    return _ffn(x, o, mod3, ln0_g, ln0_b, w_o.astype(BF16), ln1_g, ln1_b,
                ffn_w_up.astype(BF16), ffn_conv, ffn_conv_b,
                ffn_w_down.astype(BF16), ln2_g, ln2_b, tf, ff_chunks)
```

```python
import functools

import jax
import jax.numpy as jnp
import numpy as np
from jax import lax
from jax.experimental import pallas as pl
from jax.experimental.pallas import tpu as pltpu

F32 = jnp.float32
BF16 = jnp.bfloat16

D_MODEL = 1024
DN_HEADS = 4
DN_DIM = 128
DN_WIDTH = DN_HEADS * DN_DIM
SHORT_CONV = 4
GLA_HEADS = 4
GLA_KEY = 64
GLA_VAL = 128
GLA_KW = GLA_HEADS * GLA_KEY
GLA_WIDTH = GLA_HEADS * GLA_VAL
GLA_RANK = 16
GLA_TAU = 16.0
CHUNK = 64
D_FF = 2816
FFN_CONV = 3
ALPHA = 2.0 ** 0.25
EPS = 1e-6
NH = 4
QUAD = NH * CHUNK
PAIR = 2 * DN_DIM
TRI_SPAN = 256
FF_SLAB = 256
C_QKV = 0
C_Z = 3 * DN_WIDTH
C_GQ = C_Z + DN_WIDTH
C_GK = C_GQ + GLA_KW
C_GV = C_GK + GLA_KW
C_GG = C_GV + GLA_WIDTH
C_SM = C_GG + GLA_WIDTH
SMALL = 128
PROJ_W = C_SM + SMALL
SM_A, SM_B, SM_R = 0, DN_HEADS, 2 * DN_HEADS
P_Z, P_GQ, P_GK, P_GV, P_GG = 0, C_GQ - C_Z, C_GK - C_Z, C_GV - C_Z, C_GG - C_Z

BC_GQ = 0
BC_G = QUAD
BC_B = BC_G + DN_WIDTH
BC_W = BC_B + DN_WIDTH

CARRY = 8
VMEM_LIMIT = 56 * 1024 * 1024


def _dot(a, b):
    return jnp.dot(a, b, preferred_element_type=F32)


def _dot_nt(a, b):
    return lax.dot_general(a, b, (((1,), (1,)), ((), ())), preferred_element_type=F32)


def _dot_tn(a, b):
    return lax.dot_general(a, b, (((0,), (0,)), ((), ())), preferred_element_type=F32)


def _bdot(a, b):
    return _dot(a.astype(BF16), b.astype(BF16))


def _split(a):
    hi = a.astype(BF16)
    lo = (a - hi.astype(F32)).astype(BF16)
    return hi, lo


def _dot_exact_lhs(a_bf16, b):
    bh, bl = _split(b)
    return _dot(a_bf16, bh) + _dot(a_bf16, bl)


def _silu_of_twice(h):
    return h + h * jnp.tanh(h)


def _silu(x):
    return _silu_of_twice(0.5 * x)


def _shifted_taps(x_ext, n_taps, rows):
    taps = []
    for k in range(n_taps):
        shift = n_taps - 1 - k
        rolled = x_ext if shift == 0 else pltpu.roll(x_ext, shift, 0)
        taps.append(rolled[CARRY:CARRY + rows])
    return taps


def _layer_norm(x, g, b):
    mu = jnp.mean(x, axis=-1, keepdims=True)
    xc = x - mu
    var = jnp.mean(xc * xc, axis=-1, keepdims=True)
    return xc * lax.rsqrt(var + EPS) * g + b


def _rows_by_head(a, width):
    return jnp.concatenate([a[:, h * width:(h + 1) * width] for h in range(NH)], axis=0)


def _stack4(a):
    return jnp.concatenate([a, a, a, a], axis=0)


def _ada_kernel(c_ref, w_ref, b_ref, o_ref):
    cond = _silu(c_ref[...])
    o_ref[...] = _bdot(cond, w_ref[...]) + b_ref[...]


def _ada(c, w_ada, b_ada):
    bsz = c.shape[0]
    n_out = w_ada.shape[1]
    blk = D_MODEL
    return pl.pallas_call(
        _ada_kernel,
        grid=(n_out // blk,),
        in_specs=[pl.BlockSpec((bsz, D_MODEL), lambda j: (0, 0)),
                  pl.BlockSpec((D_MODEL, blk), lambda j: (0, j)),
                  pl.BlockSpec((1, blk), lambda j: (0, j))],
        out_specs=pl.BlockSpec((bsz, blk), lambda j: (0, j)),
        out_shape=jax.ShapeDtypeStruct((bsz, n_out), F32),
        name="ada_mod",
    )(c, w_ada, b_ada.reshape(1, n_out))


_DONE = object()


def _drive(tasks):
    tasks = list(tasks)
    while tasks:
        tasks = [task for task in tasks if next(task, _DONE) is not _DONE]


def _mixer_kernel(x_ref, mod_ref, g0_ref, b0_ref, win_ref, tri_ref, sel_ref, mq_ref, mr_ref,
                  cm_ref, cw_ref, alog_ref, dtb_ref, dng_ref, wg2_ref, bg_ref, glg_ref,
                  o_ref,
                  h_scr, proj_scr, xe_scr, qn_scr, kn_scr, v_scr, bc_scr, gb_scr, dec_scr,
                  nds_scr, pw_scr, q_scr, z_scr, sdn_scr, sgl_scr, *, tt):
    assert SHORT_CONV == 4
    t_idx = pl.program_id(1)
    n_chunks = tt // CHUNK

    @pl.when(t_idx == 0)
    def _():
        xe_scr[0:CARRY, :] = jnp.zeros((CARRY, 3 * DN_WIDTH), F32)
        sdn_scr[...] = jnp.zeros_like(sdn_scr)
        sgl_scr[...] = jnp.zeros_like(sgl_scr)

    x0 = _layer_norm(x_ref[0], g0_ref[...], b0_ref[...])
    h_scr[...] = (x0 * (1.0 + mod_ref[0, :, D_MODEL:2 * D_MODEL])
                  + mod_ref[0, :, 0:D_MODEL]).astype(BF16)

    def project(c0, width):
        return _dot(h_scr[...], win_ref[:, c0:c0 + width])

    small = project(C_SM, SMALL)

    for j in range(3 * DN_HEADS):
        if j % DN_HEADS == 0:
            xe_scr[CARRY:CARRY + tt, j * DN_DIM:j * DN_DIM + DN_WIDTH] = project(
                C_QKV + j * DN_DIM, DN_WIDTH)
        cs = slice(j * DN_DIM, (j + 1) * DN_DIM)
        x_ext = xe_scr[:, cs]
        x_prev = pltpu.roll(x_ext, 1, 0)
        near = x_ext * cw_ref[3:4, cs] + x_prev * cw_ref[2:3, cs]
        far = x_ext * cw_ref[1:2, cs] + x_prev * cw_ref[0:1, cs]
        conv = (near + pltpu.roll(far, 2, 0))[CARRY:CARRY + tt]
        act = _silu_of_twice(conv)
        hs = slice((j % DN_HEADS) * DN_DIM, (j % DN_HEADS + 1) * DN_DIM)
        if j < DN_HEADS:
            qn_scr[:, hs] = act * (lax.rsqrt(jnp.sum(act * act, -1, keepdims=True) + EPS)
                                   * (DN_DIM ** -0.5))
        elif j < 2 * DN_HEADS:
            kn_scr[:, hs] = act * lax.rsqrt(jnp.sum(act * act, -1, keepdims=True) + EPS)
        else:
            v_scr[:, hs] = act
    xe_scr[0:CARRY, :] = xe_scr[tt:tt + CARRY, :]
    proj_scr[...] = project(C_Z, C_SM - C_Z)

    lane = lax.broadcasted_iota(jnp.int32, (tt, SMALL), 1)
    log_a = -jnp.exp(alog_ref[...]) * jax.nn.softplus(small + dtb_ref[...])
    log_a = jnp.where(lane < DN_HEADS, log_a, 0.0)
    def chunk_cumsum(a):
        span = tri_ref.shape[0]
        return jnp.concatenate(
            [_dot_exact_lhs(tri_ref[...], a[r:r + span]) for r in range(0, tt, span)], axis=0)

    g_cum = chunk_cumsum(log_a)
    sc = jnp.where(lane < DN_HEADS, g_cum, jax.nn.sigmoid(small))
    sc_hi, sc_lo = _split(sc)
    sl = jnp.concatenate([sc_hi, sc_lo], axis=1)
    bc_scr[...] = _dot(sl, sel_ref[...])

    ones_lhs = jnp.ones((CHUNK, 2 * SMALL), BF16)
    for c in range(n_chunks):
        rows = slice(c * CHUNK, (c + 1) * CHUNK)
        g_row = _dot_nt(ones_lhs, _stack4(sl[rows]) * mr_ref[...])
        g_col = bc_scr[rows, BC_GQ:BC_GQ + QUAD]
        decay = jnp.exp(jnp.where(cm_ref[1] > 0.0, g_col - g_row, -jnp.inf))
        dec_scr[c] = decay
        nds_scr[c] = cm_ref[0] - decay

    gate_pre = _bdot(small, wg2_ref[...]) + bg_ref[...]
    log_alpha = jax.nn.log_sigmoid(gate_pre) * (1.0 / GLA_TAU)
    gb_scr[...] = chunk_cumsum(log_alpha)

    def rows_of(c):
        return slice(c * CHUNK, (c + 1) * CHUNK)

    def block_diag(a_bf16):
        zero = jnp.zeros((CHUNK, 2 * CHUNK), BF16)
        keep_lo = mq_ref[0:CHUNK, 0:2 * CHUNK]
        keep_hi = mq_ref[CHUNK:2 * CHUNK, 0:2 * CHUNK]
        left, right = a_bf16[:, 0:2 * CHUNK], a_bf16[:, 2 * CHUNK:QUAD]
        return jnp.concatenate(
            [jnp.concatenate([left * keep_lo, zero], axis=1),
             jnp.concatenate([left * keep_hi, zero], axis=1),
             jnp.concatenate([zero, right * keep_lo], axis=1),
             jnp.concatenate([zero, right * keep_hi], axis=1)], axis=0)

    def dn_prepare(c):
        rows = rows_of(c)
        kn = kn_scr[rows, :]
        qn = qn_scr[rows, :]
        vv = v_scr[rows, :]
        g_w = bc_scr[rows, BC_G:BC_G + DN_WIDTH]
        b_w = bc_scr[rows, BC_B:BC_B + DN_WIDTH]
        kb = kn * b_w
        kn_b = kn.astype(BF16)
        zero = jnp.zeros((CHUNK, DN_DIM), BF16)
        k_bd = jnp.concatenate(
            [jnp.concatenate([kn_b[:, h * DN_DIM:(h + 1) * DN_DIM] if g == h else zero
                              for g in range(NH)], axis=1) for h in range(NH)], axis=0)
        aqk = _dot_nt(jnp.concatenate([kb, qn], axis=0).astype(BF16), k_bd)
        yield
        attn = aqk[CHUNK:2 * CHUNK] * dec_scr[c]
        p = aqk[0:CHUNK] * nds_scr[c]
        t = cm_ref[0] + p
        n_levels = CHUNK.bit_length() - 1
        for lvl in range(n_levels):
            bd = block_diag(p.astype(BF16))
            if lvl == 0:
                p = _dot(p.astype(BF16), bd)
                yield
            elif lvl < n_levels - 1:
                both = _dot(jnp.concatenate([p, t], axis=0).astype(BF16), bd)
                yield
                p = both[0:CHUNK]
                t = t + both[CHUNK:2 * CHUNK]
            else:
                t_inc = _dot(t.astype(BF16), bd)
                yield
                t = t + t_inc
        eg = jnp.exp(g_w)
        kg = (kb * eg).astype(BF16)
        vb = (vv * b_w).astype(BF16)
        rhs = jnp.concatenate(
            [jnp.concatenate([kg[:, h * DN_DIM:(h + 1) * DN_DIM],
                              vb[:, h * DN_DIM:(h + 1) * DN_DIM]], axis=1) for h in range(NH)],
            axis=0)
        wu = _dot(block_diag(t.astype(BF16)), rhs)
        g_last = g_w[CHUNK - 1:CHUNK, :]
        kd = (kn * jnp.exp(g_last - g_w)).astype(BF16)
        yield
        wu = wu.astype(BF16)
        aw = _dot(block_diag(attn.astype(BF16)), wu)
        pq = [_dot_tn(kd[:, h * DN_DIM:(h + 1) * DN_DIM], wu[h * CHUNK:(h + 1) * CHUNK, :])
              for h in range(NH)]
        yield
        z_scr[c] = jnp.concatenate(
            [aw[h * CHUNK:(h + 1) * CHUNK, DN_DIM:2 * DN_DIM] for h in range(NH)], axis=1)
        q_eff = qn * eg - jnp.concatenate(
            [aw[h * CHUNK:(h + 1) * CHUNK, 0:DN_DIM] for h in range(NH)], axis=1)
        for pr in range(NH // 2):
            h0, h1 = 2 * pr, 2 * pr + 1
            pw_scr[c, pr, 0:DN_DIM, :] = jnp.concatenate(
                [pq[h0][:, 0:DN_DIM], pq[h1][:, 0:DN_DIM]], axis=1).astype(BF16)
            pw_scr[c, pr, DN_DIM:DN_DIM + CHUNK, :] = q_eff[:, pr * PAIR:(pr + 1) * PAIR].astype(BF16)
            q_scr[c, pr] = jnp.concatenate(
                [pq[h0][:, DN_DIM:2 * DN_DIM], pq[h1][:, DN_DIM:2 * DN_DIM]], axis=1)

    dn_state = [sdn_scr[pr] for pr in range(NH // 2)]

    def dn_apply(chunks):
        zero = jnp.zeros((DN_DIM, DN_DIM), BF16)
        for c in chunks:
            rows = rows_of(c)
            eg_last = jnp.exp(bc_scr[(c + 1) * CHUNK - 1:(c + 1) * CHUNK, BC_G:BC_G + DN_WIDTH])
            prods = []
            for pr in range(NH // 2):
                s_b = dn_state[pr].astype(BF16)
                rhs = jnp.concatenate(
                    [jnp.concatenate([s_b[:, 0:DN_DIM], zero], axis=1),
                     jnp.concatenate([zero, s_b[:, DN_DIM:PAIR]], axis=1)], axis=0)
                prods.append(_dot(pw_scr[c, pr], rhs))
            yield
            for pr in range(NH // 2):
                dn_state[pr] = ((dn_state[pr] * eg_last[:, pr * PAIR:(pr + 1) * PAIR]
                                 - prods[pr][0:DN_DIM]) + q_scr[c, pr])
            o = jnp.concatenate([r[DN_DIM:DN_DIM + CHUNK] for r in prods], axis=1) + z_scr[c]
            z = proj_scr[rows, P_Z:P_Z + DN_WIDTH]
            normed = []
            for h in range(NH):
                oh = o[:, h * DN_DIM:(h + 1) * DN_DIM]
                normed.append(oh * lax.rsqrt(jnp.mean(oh * oh, -1, keepdims=True) + EPS))
            on = jnp.concatenate(normed, axis=1) * dng_ref[...]
            o_ref[0, rows, 0:DN_WIDTH] = (on * _silu_of_twice(z)).astype(o_ref.dtype)
            yield

    gla_state = [sgl_scr[...]]

    def gla_chunk(c):
        rows = rows_of(c)
        b = gb_scr[rows, :]
        q = proj_scr[rows, P_GQ:P_GQ + GLA_KW] * (GLA_KEY ** -0.5)
        k = proj_scr[rows, P_GK:P_GK + GLA_KW]
        v = proj_scr[rows, P_GV:P_GV + GLA_WIDTH].astype(BF16)
        b_last = b[CHUNK - 1:CHUNK, :]
        q_dec = (q * jnp.exp(b)).astype(BF16)
        k_inv = (k * jnp.exp(-b)).astype(BF16)
        k_dec = (k * jnp.exp(b_last - b)).astype(BF16)
        v_rows = _rows_by_head(v, GLA_VAL)
        scores = _dot_nt(q_dec, block_diag(k_inv))
        kt = _dot_tn(v_rows, block_diag(k_dec))
        yield
        attn = jnp.where(cm_ref[1] > 0.0, scores, 0.0)
        st = gla_state[0]
        o = (_dot_nt(block_diag(q_dec), st.astype(BF16))
             + _dot(block_diag(attn.astype(BF16)), v_rows))
        gla_state[0] = st * jnp.exp(b_last) + kt
        yield
        gg = _rows_by_head(proj_scr[rows, P_GG:P_GG + GLA_WIDTH], GLA_VAL)
        on = o * lax.rsqrt(jnp.mean(o * o, -1, keepdims=True) + EPS) * glg_ref[...]
        res = (on * _silu_of_twice(gg)).astype(o_ref.dtype)
        for h in range(NH):
            o_ref[0, rows, DN_WIDTH + h * GLA_VAL:DN_WIDTH + (h + 1) * GLA_VAL] = (
                res[h * CHUNK:(h + 1) * CHUNK, :])

    _drive([dn_prepare(c) for c in range(n_chunks)])
    _drive([dn_apply(range(n_chunks))] + [gla_chunk(c) for c in range(n_chunks)])
    for pr in range(NH // 2):
        sdn_scr[pr] = dn_state[pr]
    sgl_scr[...] = gla_state[0]


def _mixer_constants():
    idx = np.arange(TRI_SPAN)
    tri = (idx[:, None] >= idx[None, :]) & (idx[:, None] // CHUNK == idx[None, :] // CHUNK)
    sel = np.zeros((2 * SMALL, BC_W), np.float32)
    for part in range(2):
        for h in range(NH):
            sel[part * SMALL + SM_A + h, BC_GQ + h * CHUNK:BC_GQ + (h + 1) * CHUNK] = 1.0
            sel[part * SMALL + SM_A + h, BC_G + h * DN_DIM:BC_G + (h + 1) * DN_DIM] = 1.0
            sel[part * SMALL + SM_B + h, BC_B + h * DN_DIM:BC_B + (h + 1) * DN_DIM] = 1.0
    r = np.arange(QUAD)
    mask_q = (r[:, None] // CHUNK == r[None, :] // CHUNK)
    l2 = np.arange(2 * SMALL)
    mask_r = (l2[None, :] % SMALL == SM_A + r[:, None] // CHUNK)
    i = np.arange(CHUNK)[:, None]
    j = r[None, :] % CHUNK
    quad_consts = jnp.asarray(np.stack([i == j, i >= j]).astype(np.float32))
    as_bf16 = lambda a: jnp.asarray(a.astype(np.float32), dtype=BF16)
    return as_bf16(tri), as_bf16(sel), as_bf16(mask_q), as_bf16(mask_r), quad_consts


def _mixer(x, mod3, ln0_g, ln0_b, w_in_r, dn_conv, alog_p, dtb_p, dn_norm_g, wg2_p, b_gate,
           gla_norm_g, tt):
    bsz, t, _ = x.shape
    n_chunks = tt // CHUNK
    assert tt % TRI_SPAN == 0 and n_chunks % 2 == 0
    const = lambda b, i: (0, 0)
    tri, sel, mask_q, mask_r, quad_consts = _mixer_constants()
    return pl.pallas_call(
        functools.partial(_mixer_kernel, tt=tt),
        grid=(bsz, t // tt),
        in_specs=[pl.BlockSpec((1, tt, D_MODEL), lambda b, i: (b, i, 0)),
                  pl.BlockSpec((1, 1, 6 * D_MODEL), lambda b, i: (b, 0, 0)),
                  pl.BlockSpec((1, D_MODEL), const),
                  pl.BlockSpec((1, D_MODEL), const),
                  pl.BlockSpec((D_MODEL, PROJ_W), const, pipeline_mode=pl.Buffered(1)),
                  pl.BlockSpec((TRI_SPAN, TRI_SPAN), const),
                  pl.BlockSpec((2 * SMALL, BC_W), const),
                  pl.BlockSpec((QUAD, QUAD), const),
                  pl.BlockSpec((QUAD, 2 * SMALL), const),
                  pl.BlockSpec((2, CHUNK, QUAD), lambda b, i: (0, 0, 0)),
                  pl.BlockSpec((SHORT_CONV, 3 * DN_WIDTH), const),
                  pl.BlockSpec((1, SMALL), const),
                  pl.BlockSpec((1, SMALL), const),
                  pl.BlockSpec((1, DN_WIDTH), const),
                  pl.BlockSpec((SMALL, GLA_KW), const),
                  pl.BlockSpec((1, GLA_KW), const),
                  pl.BlockSpec((1, GLA_VAL), const)],
        out_specs=pl.BlockSpec((1, tt, D_MODEL), lambda b, i: (b, i, 0)),
        out_shape=jax.ShapeDtypeStruct((bsz, t, D_MODEL), BF16),
        scratch_shapes=[pltpu.VMEM((tt, D_MODEL), BF16),
                        pltpu.VMEM((tt, C_SM - C_Z), F32),
                        pltpu.VMEM((tt + CARRY, 3 * DN_WIDTH), F32),
                        pltpu.VMEM((tt, DN_WIDTH), F32),
                        pltpu.VMEM((tt, DN_WIDTH), F32),
                        pltpu.VMEM((tt, DN_WIDTH), F32),
                        pltpu.VMEM((tt, BC_W), F32),
                        pltpu.VMEM((tt, GLA_KW), F32),
                        pltpu.VMEM((n_chunks, CHUNK, QUAD), F32),
                        pltpu.VMEM((n_chunks, CHUNK, QUAD), F32),
                        pltpu.VMEM((n_chunks, NH // 2, DN_DIM + CHUNK, PAIR), BF16),
                        pltpu.VMEM((n_chunks, NH // 2, DN_DIM, PAIR), F32),
                        pltpu.VMEM((n_chunks, CHUNK, DN_WIDTH), F32),
                        pltpu.VMEM((NH // 2, DN_DIM, PAIR), F32),
                        pltpu.VMEM((GLA_VAL, GLA_KW), F32)],
        compiler_params=pltpu.CompilerParams(
            dimension_semantics=("arbitrary", "arbitrary"), vmem_limit_bytes=VMEM_LIMIT),
        name="mixer",
    )(x, mod3, ln0_g.reshape(1, -1), ln0_b.reshape(1, -1), w_in_r, tri, sel, mask_q, mask_r,
      quad_consts, dn_conv, alog_p, dtb_p, jnp.tile(dn_norm_g.reshape(1, -1), (1, NH)), wg2_p,
      b_gate.reshape(1, -1), gla_norm_g.reshape(1, -1))


def _ffn_kernel(x_ref, o_ref, mod_ref, g0_ref, b0_ref, wo_ref, g1_ref, b1_ref,
                wup_ref, cw_ref, cb_ref, wdn_ref, g2_ref, b2_ref,
                out_ref, ue_scr, act_scr, *, tf, ff_chunks):
    t_idx = pl.program_id(1)

    @pl.when(t_idx == 0)
    def _():
        ue_scr[:, 0:CARRY, :] = jnp.zeros((ue_scr.shape[0], CARRY, ue_scr.shape[2]), F32)

    gt_a = mod_ref[0, :, 2 * D_MODEL:3 * D_MODEL]
    sh_f = mod_ref[0, :, 3 * D_MODEL:4 * D_MODEL]
    sc_f = mod_ref[0, :, 4 * D_MODEL:5 * D_MODEL]
    gt_f = mod_ref[0, :, 5 * D_MODEL:6 * D_MODEL]

    x0 = _layer_norm(x_ref[0], g0_ref[...], b0_ref[...])
    y = _dot(o_ref[0], wo_ref[...])
    x1 = _layer_norm(ALPHA * x0 + (1.0 + gt_a) * y, g1_ref[...], b1_ref[...])
    h2 = (x1 * (1.0 + sc_f) + sh_f).astype(BF16)

    y2 = None
    for j, (f0, fw) in enumerate(ff_chunks):
        for part in range(2):
            c0 = part * D_FF + f0
            ue_scr[2 * j + part, CARRY:CARRY + tf, 0:fw] = _dot(h2, wup_ref[:, c0:c0 + fw])
        for s0 in range(0, fw, FF_SLAB):
            halves = []
            for part in range(2):
                c0 = part * D_FF + f0 + s0
                slot = 2 * j + part
                acc = cb_ref[:, c0:c0 + FF_SLAB]
                taps = _shifted_taps(ue_scr[slot, :, s0:s0 + FF_SLAB], FFN_CONV, tf)
                for k, tap in enumerate(taps):
                    acc = acc + tap * cw_ref[k:k + 1, c0:c0 + FF_SLAB]
                halves.append(acc)
            act_scr[j % 2, :, s0:s0 + FF_SLAB] = (_silu_of_twice(halves[0]) * halves[1]).astype(BF16)
        for part in range(2):
            slot = 2 * j + part
            ue_scr[slot, 0:CARRY, 0:fw] = ue_scr[slot, tf:tf + CARRY, 0:fw]
        part_y = _dot(act_scr[j % 2, :, 0:fw], wdn_ref[f0:f0 + fw, :])
        y2 = part_y if y2 is None else y2 + part_y
    out_ref[0] = _layer_norm(ALPHA * x1 + (1.0 + gt_f) * y2, g2_ref[...], b2_ref[...])


def _ffn(x, o, mod3, ln0_g, ln0_b, w_o, ln1_g, ln1_b, w_up, conv_w, conv_b, w_down,
         ln2_g, ln2_b, tf, ff_chunks):
    bsz, t, _ = x.shape
    const = lambda b, i: (0, 0)
    once = pl.Buffered(1)
    vec = lambda a: a.reshape(1, -1)
    fw_max = max(fw for _, fw in ff_chunks)
    assert sum(fw for _, fw in ff_chunks) == D_FF and all(fw % FF_SLAB == 0 for _, fw in ff_chunks)
    return pl.pallas_call(
        functools.partial(_ffn_kernel, tf=tf, ff_chunks=ff_chunks),
        grid=(bsz, t // tf),
        in_specs=[pl.BlockSpec((1, tf, D_MODEL), lambda b, i: (b, i, 0)),
                  pl.BlockSpec((1, tf, D_MODEL), lambda b, i: (b, i, 0)),
                  pl.BlockSpec((1, 1, 6 * D_MODEL), lambda b, i: (b, 0, 0)),
                  pl.BlockSpec((1, D_MODEL), const),
                  pl.BlockSpec((1, D_MODEL), const),
                  pl.BlockSpec((D_MODEL, D_MODEL), const, pipeline_mode=once),
                  pl.BlockSpec((1, D_MODEL), const),
                  pl.BlockSpec((1, D_MODEL), const),
                  pl.BlockSpec((D_MODEL, 2 * D_FF), const, pipeline_mode=once),
                  pl.BlockSpec((FFN_CONV, 2 * D_FF), const),
                  pl.BlockSpec((1, 2 * D_FF), const),
                  pl.BlockSpec((D_FF, D_MODEL), const, pipeline_mode=once),
                  pl.BlockSpec((1, D_MODEL), const),
                  pl.BlockSpec((1, D_MODEL), const)],
        out_specs=pl.BlockSpec((1, tf, D_MODEL), lambda b, i: (b, i, 0)),
        out_shape=jax.ShapeDtypeStruct((bsz, t, D_MODEL), F32),
        scratch_shapes=[pltpu.VMEM((2 * len(ff_chunks), tf + CARRY, fw_max), F32),
                        pltpu.VMEM((2, tf, fw_max), BF16)],
        compiler_params=pltpu.CompilerParams(
            dimension_semantics=("arbitrary", "arbitrary"), vmem_limit_bytes=VMEM_LIMIT),
        name="ffn",
    )(x, o, mod3, vec(ln0_g), vec(ln0_b), w_o, vec(ln1_g), vec(ln1_b), w_up, conv_w,
      vec(conv_b), w_down, vec(ln2_g), vec(ln2_b))


def _pad_lanes(a, offset, width):
    a2 = a.reshape(1, -1)
    return jnp.pad(a2, ((0, 0), (offset, width - offset - a2.shape[1])))


def kernel(x, c, ln0_g, ln0_b, w_ada, b_ada, w_in, dn_conv, dn_a_log, dn_dt_bias, dn_norm_g,
           gla_w_gate2, gla_b_gate, gla_norm_g, w_o, ln1_g, ln1_b, ffn_w_up, ffn_conv, ffn_conv_b,
           ffn_w_down, ln2_g, ln2_b):
    bsz, t, _ = x.shape
    tt, tf = 512, 512
    ff_chunks = ((0, 1536), (1536, 1280))

    def layer0(a):
        assert a.shape[0] == 1
        return a.reshape(a.shape[1:])

    (w_ada, b_ada, w_in, dn_conv, dn_a_log, dn_dt_bias, dn_norm_g, gla_w_gate2, gla_b_gate,
     gla_norm_g, w_o, ln1_g, ln1_b, ffn_w_up, ffn_conv, ffn_conv_b, ffn_w_down, ln2_g, ln2_b) = map(
        layer0, (w_ada, b_ada, w_in, dn_conv, dn_a_log, dn_dt_bias, dn_norm_g, gla_w_gate2,
                 gla_b_gate, gla_norm_g, w_o, ln1_g, ln1_b, ffn_w_up, ffn_conv, ffn_conv_b,
                 ffn_w_down, ln2_g, ln2_b))
    wi = w_in
    o_q = 0
    o_a = 4 * DN_WIDTH
    o_b = o_a + DN_HEADS
    o_gq = o_b + DN_HEADS
    o_gr = o_gq + 2 * GLA_KW + 2 * GLA_WIDTH
    small_cols = jnp.concatenate(
        [wi[:, o_a:o_a + DN_HEADS], wi[:, o_b:o_b + DN_HEADS], wi[:, o_gr:o_gr + GLA_RANK],
         jnp.zeros((D_MODEL, SMALL - 2 * DN_HEADS - GLA_RANK), wi.dtype)], axis=1)
    o_z = 3 * DN_WIDTH
    o_gg = o_gr - GLA_WIDTH
    w_in_r = jnp.concatenate(
        [wi[:, o_q:o_z], 0.5 * wi[:, o_z:o_a], wi[:, o_gq:o_gg], 0.5 * wi[:, o_gg:o_gr], small_cols],
        axis=1).astype(BF16)
    dn_conv = 0.5 * dn_conv
    gate_half = jnp.concatenate([jnp.full((D_FF,), 0.5, F32), jnp.ones((D_FF,), F32)])
    ffn_conv = ffn_conv * gate_half
    ffn_conv_b = ffn_conv_b * gate_half
    alog_p = _pad_lanes(dn_a_log, SM_A, SMALL)
    dtb_p = _pad_lanes(dn_dt_bias, SM_A, SMALL)
    wg2_p = jnp.pad(gla_w_gate2, ((SM_R, SMALL - SM_R - GLA_RANK), (0, 0))).astype(BF16)

    mod = _ada(c, w_ada, b_ada)
    mod3 = mod.reshape(bsz, 1, 6 * D_MODEL)
    o = _mixer(x, mod3, ln0_g, ln0_b, w_in_r, dn_conv, alog_p, dtb_p, dn_norm_g, wg2_p,
               gla_b_gate, gla_norm_g, tt)
    return _ffn(x, o, mod3, ln0_g, ln0_b, w_o.astype(BF16), ln1_g, ln1_b,
                ffn_w_up.astype(BF16), ffn_conv, ffn_conv_b,
                ffn_w_down.astype(BF16), ln2_g, ln2_b, tf, ff_chunks)
```

```python
import functools

import jax
import jax.numpy as jnp
import numpy as np
from jax import lax
from jax.experimental import pallas as pl
from jax.experimental.pallas import tpu as pltpu

F32 = jnp.float32
BF16 = jnp.bfloat16

D_MODEL = 1024
DN_HEADS = 4
DN_DIM = 128
DN_WIDTH = DN_HEADS * DN_DIM
SHORT_CONV = 4
GLA_HEADS = 4
GLA_KEY = 64
GLA_VAL = 128
GLA_KW = GLA_HEADS * GLA_KEY
GLA_WIDTH = GLA_HEADS * GLA_VAL
GLA_RANK = 16
GLA_TAU = 16.0
CHUNK = 64
D_FF = 2816
FFN_CONV = 3
ALPHA = 2.0 ** 0.25
EPS = 1e-6
NH = 4
QUAD = NH * CHUNK
PAIR = 2 * DN_DIM
TRI_SPAN = 256
FF_SLAB = 256
C_QKV = 0
C_Z = 3 * DN_WIDTH
C_GQ = C_Z + DN_WIDTH
C_GK = C_GQ + GLA_KW
C_GV = C_GK + GLA_KW
C_GG = C_GV + GLA_WIDTH
C_SM = C_GG + GLA_WIDTH
SMALL = 128
PROJ_W = C_SM + SMALL
SM_A, SM_B, SM_R = 0, DN_HEADS, 2 * DN_HEADS
P_Z, P_GQ, P_GK, P_GV, P_GG = 0, C_GQ - C_Z, C_GK - C_Z, C_GV - C_Z, C_GG - C_Z

BC_GQ = 0
BC_G = QUAD
BC_B = BC_G + DN_WIDTH
BC_W = BC_B + DN_WIDTH

CARRY = 8
VMEM_LIMIT = 56 * 1024 * 1024


def _dot(a, b):
    return jnp.dot(a, b, preferred_element_type=F32)


def _dot_nt(a, b):
    return lax.dot_general(a, b, (((1,), (1,)), ((), ())), preferred_element_type=F32)


def _dot_tn(a, b):
    return lax.dot_general(a, b, (((0,), (0,)), ((), ())), preferred_element_type=F32)


def _bdot(a, b):
    return _dot(a.astype(BF16), b.astype(BF16))


def _split(a):
    hi = a.astype(BF16)
    lo = (a - hi.astype(F32)).astype(BF16)
    return hi, lo


def _dot_exact_lhs(a_bf16, b):
    bh, bl = _split(b)
    return _dot(a_bf16, bh) + _dot(a_bf16, bl)


def _silu_of_twice(h):
    return h + h * jnp.tanh(h)


def _silu(x):
    return _silu_of_twice(0.5 * x)


def _shifted_taps(x_ext, n_taps, rows):
    taps = []
    for k in range(n_taps):
        shift = n_taps - 1 - k
        rolled = x_ext if shift == 0 else pltpu.roll(x_ext, shift, 0)
        taps.append(rolled[CARRY:CARRY + rows])
    return taps


def _layer_norm(x, g, b):
    mu = jnp.mean(x, axis=-1, keepdims=True)
    xc = x - mu
    var = jnp.mean(xc * xc, axis=-1, keepdims=True)
    return xc * lax.rsqrt(var + EPS) * g + b


def _rows_by_head(a, width):
    return jnp.concatenate([a[:, h * width:(h + 1) * width] for h in range(NH)], axis=0)


def _stack4(a):
    return jnp.concatenate([a, a, a, a], axis=0)


def _ada_kernel(c_ref, w_ref, b_ref, o_ref):
    cond = _silu(c_ref[...])
    o_ref[...] = _bdot(cond, w_ref[...]) + b_ref[...]


def _ada(c, w_ada, b_ada):
    bsz = c.shape[0]
    n_out = w_ada.shape[1]
    blk = D_MODEL
    return pl.pallas_call(
        _ada_kernel,
        grid=(n_out // blk,),
        in_specs=[pl.BlockSpec((bsz, D_MODEL), lambda j: (0, 0)),
                  pl.BlockSpec((D_MODEL, blk), lambda j: (0, j)),
                  pl.BlockSpec((1, blk), lambda j: (0, j))],
        out_specs=pl.BlockSpec((bsz, blk), lambda j: (0, j)),
        out_shape=jax.ShapeDtypeStruct((bsz, n_out), F32),
        name="ada_mod",
    )(c, w_ada, b_ada.reshape(1, n_out))


_DONE = object()


def _drive(tasks):
    tasks = list(tasks)
    while tasks:
        tasks = [task for task in tasks if next(task, _DONE) is not _DONE]


def _mixer_kernel(x_ref, mod_ref, g0_ref, b0_ref, win_ref, tri_ref, sel_ref, mq_ref, mr_ref,
                  cm_ref, cw_ref, alog_ref, dtb_ref, dng_ref, wg2_ref, bg_ref, glg_ref,
                  o_ref,
                  h_scr, proj_scr, xe_scr, qn_scr, kn_scr, v_scr, bc_scr, gb_scr, dec_scr,
                  nds_scr, pw_scr, q_scr, z_scr, sdn_scr, sgl_scr, raw_scr, *, tt):
    assert SHORT_CONV == 4
    t_idx = pl.program_id(1)
    n_chunks = tt // CHUNK

    @pl.when(t_idx == 0)
    def _():
        xe_scr[0:CARRY, :] = jnp.zeros((CARRY, 3 * DN_WIDTH), F32)
        sdn_scr[...] = jnp.zeros_like(sdn_scr)
        sgl_scr[...] = jnp.zeros_like(sgl_scr)

    x0 = _layer_norm(x_ref[0], g0_ref[...], b0_ref[...])
    h_scr[...] = (x0 * (1.0 + mod_ref[0, :, D_MODEL:2 * D_MODEL])
                  + mod_ref[0, :, 0:D_MODEL]).astype(BF16)

    def project(c0, width):
        return _dot(h_scr[...], win_ref[:, c0:c0 + width])

    small = project(C_SM, SMALL)

    for j in range(3 * DN_HEADS):
        if j % DN_HEADS == 0:
            xe_scr[CARRY:CARRY + tt, j * DN_DIM:j * DN_DIM + DN_WIDTH] = project(
                C_QKV + j * DN_DIM, DN_WIDTH)
        cs = slice(j * DN_DIM, (j + 1) * DN_DIM)
        x_ext = xe_scr[:, cs]
        x_prev = pltpu.roll(x_ext, 1, 0)
        near = x_ext * cw_ref[3:4, cs] + x_prev * cw_ref[2:3, cs]
        far = x_ext * cw_ref[1:2, cs] + x_prev * cw_ref[0:1, cs]
        conv = (near + pltpu.roll(far, 2, 0))[CARRY:CARRY + tt]
        act = _silu_of_twice(conv)
        hs = slice((j % DN_HEADS) * DN_DIM, (j % DN_HEADS + 1) * DN_DIM)
        if j < DN_HEADS:
            qn_scr[:, hs] = act * (lax.rsqrt(jnp.sum(act * act, -1, keepdims=True) + EPS)
                                   * (DN_DIM ** -0.5))
        elif j < 2 * DN_HEADS:
            kn_scr[:, hs] = act * lax.rsqrt(jnp.sum(act * act, -1, keepdims=True) + EPS)
        else:
            v_scr[:, hs] = act
    xe_scr[0:CARRY, :] = xe_scr[tt:tt + CARRY, :]
    proj_scr[...] = project(C_Z, C_SM - C_Z)

    lane = lax.broadcasted_iota(jnp.int32, (tt, SMALL), 1)
    log_a = -jnp.exp(alog_ref[...]) * jax.nn.softplus(small + dtb_ref[...])
    log_a = jnp.where(lane < DN_HEADS, log_a, 0.0)
    def chunk_cumsum(a):
        span = tri_ref.shape[0]
        return jnp.concatenate(
            [_dot_exact_lhs(tri_ref[...], a[r:r + span]) for r in range(0, tt, span)], axis=0)

    g_cum = chunk_cumsum(log_a)
    sc = jnp.where(lane < DN_HEADS, g_cum, jax.nn.sigmoid(small))
    sc_hi, sc_lo = _split(sc)
    sl = jnp.concatenate([sc_hi, sc_lo], axis=1)
    bc_scr[...] = _dot(sl, sel_ref[...])

    ones_lhs = jnp.ones((CHUNK, 2 * SMALL), BF16)
    for c in range(n_chunks):
        rows = slice(c * CHUNK, (c + 1) * CHUNK)
        g_row = _dot_nt(ones_lhs, _stack4(sl[rows]) * mr_ref[...])
        g_col = bc_scr[rows, BC_GQ:BC_GQ + QUAD]
        decay = jnp.exp(jnp.where(cm_ref[1] > 0.0, g_col - g_row, -jnp.inf))
        dec_scr[c] = decay
        nds_scr[c] = cm_ref[0] - decay

    gate_pre = _bdot(small, wg2_ref[...]) + bg_ref[...]
    log_alpha = jax.nn.log_sigmoid(gate_pre) * (1.0 / GLA_TAU)
    gb_scr[...] = chunk_cumsum(log_alpha)

    def rows_of(c):
        return slice(c * CHUNK, (c + 1) * CHUNK)

    def block_diag(a_bf16):
        zero = jnp.zeros((CHUNK, 2 * CHUNK), BF16)
        keep_lo = mq_ref[0:CHUNK, 0:2 * CHUNK]
        keep_hi = mq_ref[CHUNK:2 * CHUNK, 0:2 * CHUNK]
        left, right = a_bf16[:, 0:2 * CHUNK], a_bf16[:, 2 * CHUNK:QUAD]
        return jnp.concatenate(
            [jnp.concatenate([left * keep_lo, zero], axis=1),
             jnp.concatenate([left * keep_hi, zero], axis=1),
             jnp.concatenate([zero, right * keep_lo], axis=1),
             jnp.concatenate([zero, right * keep_hi], axis=1)], axis=0)

    def dn_prepare(c):
        rows = rows_of(c)
        kn = kn_scr[rows, :]
        qn = qn_scr[rows, :]
        vv = v_scr[rows, :]
        g_w = bc_scr[rows, BC_G:BC_G + DN_WIDTH]
        b_w = bc_scr[rows, BC_B:BC_B + DN_WIDTH]
        kb = kn * b_w
        kn_b = kn.astype(BF16)
        zero = jnp.zeros((CHUNK, DN_DIM), BF16)
        k_bd = jnp.concatenate(
            [jnp.concatenate([kn_b[:, h * DN_DIM:(h + 1) * DN_DIM] if g == h else zero
                              for g in range(NH)], axis=1) for h in range(NH)], axis=0)
        aqk = _dot_nt(jnp.concatenate([kb, qn], axis=0).astype(BF16), k_bd)
        yield
        attn = aqk[CHUNK:2 * CHUNK] * dec_scr[c]
        p = aqk[0:CHUNK] * nds_scr[c]
        t = cm_ref[0] + p
        n_levels = CHUNK.bit_length() - 1
        for lvl in range(n_levels):
            bd = block_diag(p.astype(BF16))
            if lvl == 0:
                p = _dot(p.astype(BF16), bd)
                yield
            elif lvl < n_levels - 1:
                both = _dot(jnp.concatenate([p, t], axis=0).astype(BF16), bd)
                yield
                p = both[0:CHUNK]
                t = t + both[CHUNK:2 * CHUNK]
            else:
                t_inc = _dot(t.astype(BF16), bd)
                yield
                t = t + t_inc
        eg = jnp.exp(g_w)
        kg = (kb * eg).astype(BF16)
        vb = (vv * b_w).astype(BF16)
        rhs = jnp.concatenate(
            [jnp.concatenate([kg[:, h * DN_DIM:(h + 1) * DN_DIM],
                              vb[:, h * DN_DIM:(h + 1) * DN_DIM]], axis=1) for h in range(NH)],
            axis=0)
        wu = _dot(block_diag(t.astype(BF16)), rhs)
        g_last = g_w[CHUNK - 1:CHUNK, :]
        kd = (kn * jnp.exp(g_last - g_w)).astype(BF16)
        yield
        wu = wu.astype(BF16)
        aw = _dot(block_diag(attn.astype(BF16)), wu)
        pq = [_dot_tn(kd[:, h * DN_DIM:(h + 1) * DN_DIM], wu[h * CHUNK:(h + 1) * CHUNK, :])
              for h in range(NH)]
        yield
        z_scr[c] = jnp.concatenate(
            [aw[h * CHUNK:(h + 1) * CHUNK, DN_DIM:2 * DN_DIM] for h in range(NH)], axis=1)
        q_eff = qn * eg - jnp.concatenate(
            [aw[h * CHUNK:(h + 1) * CHUNK, 0:DN_DIM] for h in range(NH)], axis=1)
        for pr in range(NH // 2):
            h0, h1 = 2 * pr, 2 * pr + 1
            pw_scr[c, pr, 0:DN_DIM, :] = jnp.concatenate(
                [pq[h0][:, 0:DN_DIM], pq[h1][:, 0:DN_DIM]], axis=1).astype(BF16)
            pw_scr[c, pr, DN_DIM:DN_DIM + CHUNK, :] = q_eff[:, pr * PAIR:(pr + 1) * PAIR].astype(BF16)
            q_scr[c, pr] = jnp.concatenate(
                [pq[h0][:, DN_DIM:2 * DN_DIM], pq[h1][:, DN_DIM:2 * DN_DIM]], axis=1)

    dn_state = [sdn_scr[pr] for pr in range(NH // 2)]

    def dn_apply(chunks):
        zero = jnp.zeros((DN_DIM, DN_DIM), BF16)
        for c in chunks:
            rows = rows_of(c)
            eg_last = jnp.exp(bc_scr[(c + 1) * CHUNK - 1:(c + 1) * CHUNK, BC_G:BC_G + DN_WIDTH])
            prods = []
            for pr in range(NH // 2):
                s_b = dn_state[pr].astype(BF16)
                rhs = jnp.concatenate(
                    [jnp.concatenate([s_b[:, 0:DN_DIM], zero], axis=1),
                     jnp.concatenate([zero, s_b[:, DN_DIM:PAIR]], axis=1)], axis=0)
                prods.append(_dot(pw_scr[c, pr], rhs))
            yield
            for pr in range(NH // 2):
                dn_state[pr] = ((dn_state[pr] * eg_last[:, pr * PAIR:(pr + 1) * PAIR]
                                 - prods[pr][0:DN_DIM]) + q_scr[c, pr])
            raw_scr[rows, 0:DN_WIDTH] = (
                jnp.concatenate([r[DN_DIM:DN_DIM + CHUNK] for r in prods], axis=1) + z_scr[c])
            yield

    gla_state = [sgl_scr[...]]

    def gla_chunk(c):
        rows = rows_of(c)
        b = gb_scr[rows, :]
        q = proj_scr[rows, P_GQ:P_GQ + GLA_KW] * (GLA_KEY ** -0.5)
        k = proj_scr[rows, P_GK:P_GK + GLA_KW]
        v = proj_scr[rows, P_GV:P_GV + GLA_WIDTH].astype(BF16)
        b_last = b[CHUNK - 1:CHUNK, :]
        q_dec = (q * jnp.exp(b)).astype(BF16)
        k_inv = (k * jnp.exp(-b)).astype(BF16)
        k_dec = (k * jnp.exp(b_last - b)).astype(BF16)
        v_rows = _rows_by_head(v, GLA_VAL)
        scores = _dot_nt(q_dec, block_diag(k_inv))
        kt = _dot_tn(v_rows, block_diag(k_dec))
        yield
        attn = jnp.where(cm_ref[1] > 0.0, scores, 0.0)
        st = gla_state[0]
        o = (_dot_nt(block_diag(q_dec), st.astype(BF16))
             + _dot(block_diag(attn.astype(BF16)), v_rows))
        gla_state[0] = st * jnp.exp(b_last) + kt
        for h in range(NH):
            raw_scr[rows, DN_WIDTH + h * GLA_VAL:DN_WIDTH + (h + 1) * GLA_VAL] = (
                o[h * CHUNK:(h + 1) * CHUNK, :])

    _drive([dn_prepare(c) for c in range(n_chunks)])
    _drive([dn_apply(range(n_chunks))] + [gla_chunk(c) for c in range(n_chunks)])
    for pr in range(NH // 2):
        sdn_scr[pr] = dn_state[pr]
    sgl_scr[...] = gla_state[0]

    for hb in range(2 * NH):
        lanes = slice(hb * DN_DIM, (hb + 1) * DN_DIM)
        oh = raw_scr[:, lanes]
        if hb < NH:
            gain = dng_ref[:, 0:DN_DIM]
            gate = proj_scr[:, P_Z + hb * DN_DIM:P_Z + (hb + 1) * DN_DIM]
        else:
            gain = glg_ref[...]
            gate = proj_scr[:, P_GG + (hb - NH) * GLA_VAL:P_GG + (hb - NH + 1) * GLA_VAL]
        on = oh * lax.rsqrt(jnp.mean(oh * oh, -1, keepdims=True) + EPS) * gain
        o_ref[0, :, lanes] = (on * _silu_of_twice(gate)).astype(o_ref.dtype)


def _mixer_constants():
    idx = np.arange(TRI_SPAN)
    tri = (idx[:, None] >= idx[None, :]) & (idx[:, None] // CHUNK == idx[None, :] // CHUNK)
    sel = np.zeros((2 * SMALL, BC_W), np.float32)
    for part in range(2):
        for h in range(NH):
            sel[part * SMALL + SM_A + h, BC_GQ + h * CHUNK:BC_GQ + (h + 1) * CHUNK] = 1.0
            sel[part * SMALL + SM_A + h, BC_G + h * DN_DIM:BC_G + (h + 1) * DN_DIM] = 1.0
            sel[part * SMALL + SM_B + h, BC_B + h * DN_DIM:BC_B + (h + 1) * DN_DIM] = 1.0
    r = np.arange(QUAD)
    mask_q = (r[:, None] // CHUNK == r[None, :] // CHUNK)
    l2 = np.arange(2 * SMALL)
    mask_r = (l2[None, :] % SMALL == SM_A + r[:, None] // CHUNK)
    i = np.arange(CHUNK)[:, None]
    j = r[None, :] % CHUNK
    quad_consts = jnp.asarray(np.stack([i == j, i >= j]).astype(np.float32))
    as_bf16 = lambda a: jnp.asarray(a.astype(np.float32), dtype=BF16)
    return as_bf16(tri), as_bf16(sel), as_bf16(mask_q), as_bf16(mask_r), quad_consts


def _mixer(x, mod3, ln0_g, ln0_b, w_in_r, dn_conv, alog_p, dtb_p, dn_norm_g, wg2_p, b_gate,
           gla_norm_g, tt):
    bsz, t, _ = x.shape
    n_chunks = tt // CHUNK
    assert tt % TRI_SPAN == 0 and n_chunks % 2 == 0
    const = lambda b, i: (0, 0)
    tri, sel, mask_q, mask_r, quad_consts = _mixer_constants()
    return pl.pallas_call(
        functools.partial(_mixer_kernel, tt=tt),
        grid=(bsz, t // tt),
        in_specs=[pl.BlockSpec((1, tt, D_MODEL), lambda b, i: (b, i, 0)),
                  pl.BlockSpec((1, 1, 6 * D_MODEL), lambda b, i: (b, 0, 0)),
                  pl.BlockSpec((1, D_MODEL), const),
                  pl.BlockSpec((1, D_MODEL), const),
                  pl.BlockSpec((D_MODEL, PROJ_W), const, pipeline_mode=pl.Buffered(1)),
                  pl.BlockSpec((TRI_SPAN, TRI_SPAN), const),
                  pl.BlockSpec((2 * SMALL, BC_W), const),
                  pl.BlockSpec((QUAD, QUAD), const),
                  pl.BlockSpec((QUAD, 2 * SMALL), const),
                  pl.BlockSpec((2, CHUNK, QUAD), lambda b, i: (0, 0, 0)),
                  pl.BlockSpec((SHORT_CONV, 3 * DN_WIDTH), const),
                  pl.BlockSpec((1, SMALL), const),
                  pl.BlockSpec((1, SMALL), const),
                  pl.BlockSpec((1, DN_WIDTH), const),
                  pl.BlockSpec((SMALL, GLA_KW), const),
                  pl.BlockSpec((1, GLA_KW), const),
                  pl.BlockSpec((1, GLA_VAL), const)],
        out_specs=pl.BlockSpec((1, tt, D_MODEL), lambda b, i: (b, i, 0)),
        out_shape=jax.ShapeDtypeStruct((bsz, t, D_MODEL), BF16),
        scratch_shapes=[pltpu.VMEM((tt, D_MODEL), BF16),
                        pltpu.VMEM((tt, C_SM - C_Z), F32),
                        pltpu.VMEM((tt + CARRY, 3 * DN_WIDTH), F32),
                        pltpu.VMEM((tt, DN_WIDTH), F32),
                        pltpu.VMEM((tt, DN_WIDTH), F32),
                        pltpu.VMEM((tt, DN_WIDTH), F32),
                        pltpu.VMEM((tt, BC_W), F32),
                        pltpu.VMEM((tt, GLA_KW), F32),
                        pltpu.VMEM((n_chunks, CHUNK, QUAD), F32),
                        pltpu.VMEM((n_chunks, CHUNK, QUAD), F32),
                        pltpu.VMEM((n_chunks, NH // 2, DN_DIM + CHUNK, PAIR), BF16),
                        pltpu.VMEM((n_chunks, NH // 2, DN_DIM, PAIR), F32),
                        pltpu.VMEM((n_chunks, CHUNK, DN_WIDTH), F32),
                        pltpu.VMEM((NH // 2, DN_DIM, PAIR), F32),
                        pltpu.VMEM((GLA_VAL, GLA_KW), F32),
                        pltpu.VMEM((tt, D_MODEL), F32)],
        compiler_params=pltpu.CompilerParams(
            dimension_semantics=("arbitrary", "arbitrary"), vmem_limit_bytes=VMEM_LIMIT),
        name="mixer",
    )(x, mod3, ln0_g.reshape(1, -1), ln0_b.reshape(1, -1), w_in_r, tri, sel, mask_q, mask_r,
      quad_consts, dn_conv, alog_p, dtb_p, jnp.tile(dn_norm_g.reshape(1, -1), (1, NH)), wg2_p,
      b_gate.reshape(1, -1), gla_norm_g.reshape(1, -1))


def _ffn_kernel(x_ref, o_ref, mod_ref, g0_ref, b0_ref, wo_ref, g1_ref, b1_ref,
                wup_ref, cw_ref, cb_ref, wdn_ref, g2_ref, b2_ref,
                out_ref, ue_scr, act_scr, *, tf, ff_chunks):
    t_idx = pl.program_id(1)

    @pl.when(t_idx == 0)
    def _():
        ue_scr[:, 0:CARRY, :] = jnp.zeros((ue_scr.shape[0], CARRY, ue_scr.shape[2]), F32)

    gt_a = mod_ref[0, :, 2 * D_MODEL:3 * D_MODEL]
    sh_f = mod_ref[0, :, 3 * D_MODEL:4 * D_MODEL]
    sc_f = mod_ref[0, :, 4 * D_MODEL:5 * D_MODEL]
    gt_f = mod_ref[0, :, 5 * D_MODEL:6 * D_MODEL]

    x0 = _layer_norm(x_ref[0], g0_ref[...], b0_ref[...])
    y = _dot(o_ref[0], wo_ref[...])
    x1 = _layer_norm(ALPHA * x0 + (1.0 + gt_a) * y, g1_ref[...], b1_ref[...])
    h2 = (x1 * (1.0 + sc_f) + sh_f).astype(BF16)

    y2 = None
    for j, (f0, fw) in enumerate(ff_chunks):
        for part in range(2):
            c0 = part * D_FF + f0
            ue_scr[2 * j + part, CARRY:CARRY + tf, 0:fw] = _dot(h2, wup_ref[:, c0:c0 + fw])
        for s0 in range(0, fw, FF_SLAB):
            halves = []
            for part in range(2):
                c0 = part * D_FF + f0 + s0
                slot = 2 * j + part
                acc = cb_ref[:, c0:c0 + FF_SLAB]
                taps = _shifted_taps(ue_scr[slot, :, s0:s0 + FF_SLAB], FFN_CONV, tf)
                for k, tap in enumerate(taps):
                    acc = acc + tap * cw_ref[k:k + 1, c0:c0 + FF_SLAB]
                halves.append(acc)
            act_scr[j % 2, :, s0:s0 + FF_SLAB] = (_silu_of_twice(halves[0]) * halves[1]).astype(BF16)
        for part in range(2):
            slot = 2 * j + part
            ue_scr[slot, 0:CARRY, 0:fw] = ue_scr[slot, tf:tf + CARRY, 0:fw]
        part_y = _dot(act_scr[j % 2, :, 0:fw], wdn_ref[f0:f0 + fw, :])
        y2 = part_y if y2 is None else y2 + part_y
    out_ref[0] = _layer_norm(ALPHA * x1 + (1.0 + gt_f) * y2, g2_ref[...], b2_ref[...])


def _ffn(x, o, mod3, ln0_g, ln0_b, w_o, ln1_g, ln1_b, w_up, conv_w, conv_b, w_down,
         ln2_g, ln2_b, tf, ff_chunks):
    bsz, t, _ = x.shape
    const = lambda b, i: (0, 0)
    once = pl.Buffered(1)
    vec = lambda a: a.reshape(1, -1)
    fw_max = max(fw for _, fw in ff_chunks)
    assert sum(fw for _, fw in ff_chunks) == D_FF and all(fw % FF_SLAB == 0 for _, fw in ff_chunks)
    return pl.pallas_call(
        functools.partial(_ffn_kernel, tf=tf, ff_chunks=ff_chunks),
        grid=(bsz, t // tf),
        in_specs=[pl.BlockSpec((1, tf, D_MODEL), lambda b, i: (b, i, 0)),
                  pl.BlockSpec((1, tf, D_MODEL), lambda b, i: (b, i, 0)),
                  pl.BlockSpec((1, 1, 6 * D_MODEL), lambda b, i: (b, 0, 0)),
                  pl.BlockSpec((1, D_MODEL), const),
                  pl.BlockSpec((1, D_MODEL), const),
                  pl.BlockSpec((D_MODEL, D_MODEL), const, pipeline_mode=once),
                  pl.BlockSpec((1, D_MODEL), const),
                  pl.BlockSpec((1, D_MODEL), const),
                  pl.BlockSpec((D_MODEL, 2 * D_FF), const, pipeline_mode=once),
                  pl.BlockSpec((FFN_CONV, 2 * D_FF), const),
                  pl.BlockSpec((1, 2 * D_FF), const),
                  pl.BlockSpec((D_FF, D_MODEL), const, pipeline_mode=once),
                  pl.BlockSpec((1, D_MODEL), const),
                  pl.BlockSpec((1, D_MODEL), const)],
        out_specs=pl.BlockSpec((1, tf, D_MODEL), lambda b, i: (b, i, 0)),
        out_shape=jax.ShapeDtypeStruct((bsz, t, D_MODEL), F32),
        scratch_shapes=[pltpu.VMEM((2 * len(ff_chunks), tf + CARRY, fw_max), F32),
                        pltpu.VMEM((2, tf, fw_max), BF16)],
        compiler_params=pltpu.CompilerParams(
            dimension_semantics=("arbitrary", "arbitrary"), vmem_limit_bytes=VMEM_LIMIT),
        name="ffn",
    )(x, o, mod3, vec(ln0_g), vec(ln0_b), w_o, vec(ln1_g), vec(ln1_b), w_up, conv_w,
      vec(conv_b), w_down, vec(ln2_g), vec(ln2_b))


def _pad_lanes(a, offset, width):
    a2 = a.reshape(1, -1)
    return jnp.pad(a2, ((0, 0), (offset, width - offset - a2.shape[1])))


def kernel(x, c, ln0_g, ln0_b, w_ada, b_ada, w_in, dn_conv, dn_a_log, dn_dt_bias, dn_norm_g,
           gla_w_gate2, gla_b_gate, gla_norm_g, w_o, ln1_g, ln1_b, ffn_w_up, ffn_conv, ffn_conv_b,
           ffn_w_down, ln2_g, ln2_b):
    bsz, t, _ = x.shape
    tt, tf = 512, 512
    ff_chunks = ((0, 1536), (1536, 1280))

    def layer0(a):
        assert a.shape[0] == 1
        return a.reshape(a.shape[1:])

    (w_ada, b_ada, w_in, dn_conv, dn_a_log, dn_dt_bias, dn_norm_g, gla_w_gate2, gla_b_gate,
     gla_norm_g, w_o, ln1_g, ln1_b, ffn_w_up, ffn_conv, ffn_conv_b, ffn_w_down, ln2_g, ln2_b) = map(
        layer0, (w_ada, b_ada, w_in, dn_conv, dn_a_log, dn_dt_bias, dn_norm_g, gla_w_gate2,
                 gla_b_gate, gla_norm_g, w_o, ln1_g, ln1_b, ffn_w_up, ffn_conv, ffn_conv_b,
                 ffn_w_down, ln2_g, ln2_b))
    wi = w_in
    o_q = 0
    o_a = 4 * DN_WIDTH
    o_b = o_a + DN_HEADS
    o_gq = o_b + DN_HEADS
    o_gr = o_gq + 2 * GLA_KW + 2 * GLA_WIDTH
    small_cols = jnp.concatenate(
        [wi[:, o_a:o_a + DN_HEADS], wi[:, o_b:o_b + DN_HEADS], wi[:, o_gr:o_gr + GLA_RANK],
         jnp.zeros((D_MODEL, SMALL - 2 * DN_HEADS - GLA_RANK), wi.dtype)], axis=1)
    o_z = 3 * DN_WIDTH
    o_gg = o_gr - GLA_WIDTH
    w_in_r = jnp.concatenate(
        [wi[:, o_q:o_z], 0.5 * wi[:, o_z:o_a], wi[:, o_gq:o_gg], 0.5 * wi[:, o_gg:o_gr], small_cols],
        axis=1).astype(BF16)
    dn_conv = 0.5 * dn_conv
    gate_half = jnp.concatenate([jnp.full((D_FF,), 0.5, F32), jnp.ones((D_FF,), F32)])
    ffn_conv = ffn_conv * gate_half
    ffn_conv_b = ffn_conv_b * gate_half
    alog_p = _pad_lanes(dn_a_log, SM_A, SMALL)
    dtb_p = _pad_lanes(dn_dt_bias, SM_A, SMALL)
    wg2_p = jnp.pad(gla_w_gate2, ((SM_R, SMALL - SM_R - GLA_RANK), (0, 0))).astype(BF16)

    mod = _ada(c, w_ada, b_ada)
    mod3 = mod.reshape(bsz, 1, 6 * D_MODEL)
    o = _mixer(x, mod3, ln0_g, ln0_b, w_in_r, dn_conv, alog_p, dtb_p, dn_norm_g, wg2_p,
               gla_b_gate, gla_norm_g, tt)
    return _ffn(x, o, mod3, ln0_g, ln0_b, w_o.astype(BF16), ln1_g, ln1_b,
                ffn_w_up.astype(BF16), ffn_conv, ffn_conv_b,
                ffn_w_down.astype(BF16), ln2_g, ln2_b, tf, ff_chunks)
```
